```python
import jax, jax.numpy as jnp
from jax import lax
import numpy as np

D_MODEL = 4096
BATCH = 1
SEQ = 8192
DEPTH = 1
DEC_BATCH = 8
DEC_SEQ = 2048
PAST_LEN = 128

LRU_WIDTH = D_MODEL
LRU_HEADS = 16
LRU_BLOCK = LRU_WIDTH // LRU_HEADS
CONV_WIDTH = 4
CONV_LEFT = 2
LRU_C = 8.0
MLA_HEADS = 32
Q_LORA = 1024
KV_LORA = 512
QK_NOPE = 128
QK_ROPE = 64
V_HEAD = 128
QK_HEAD = QK_NOPE + QK_ROPE
MLA_WIDTH = MLA_HEADS * V_HEAD
ROPE_THETA = 10000.0
Q_BLOCK = 128
D_FF = 4 * D_MODEL
N_BRANCH = 2
EPS = 1e-6
IN_COLS = 2 * LRU_WIDTH + Q_LORA + KV_LORA + QK_ROPE + N_BRANCH * D_MODEL
IN_SPLITS = (LRU_WIDTH,
             2 * LRU_WIDTH,
             2 * LRU_WIDTH + Q_LORA,
             2 * LRU_WIDTH + Q_LORA + KV_LORA,
             2 * LRU_WIDTH + Q_LORA + KV_LORA + QK_ROPE)

kernel_name = "hybrid_rglru_mla_encoder"


def rmsnorm(x, g):
    xf = x.astype(jnp.float32)
    y = xf * lax.rsqrt(jnp.mean(jnp.square(xf), axis=-1, keepdims=True) + EPS)
    return (y * g.astype(jnp.float32)).astype(x.dtype)


def rope_tables(seq):
    inv = 1.0 / (ROPE_THETA ** (jnp.arange(0, QK_ROPE, 2, dtype=jnp.float32) / QK_ROPE))
    ang = jnp.arange(seq, dtype=jnp.float32)[:, None] * inv[None, :]
    return jnp.cos(ang), jnp.sin(ang)


def apply_rope(x, cos, sin):
    xf = x.astype(jnp.float32)
    x1, x2 = xf[..., :QK_ROPE // 2], xf[..., QK_ROPE // 2:]
    return jnp.concatenate([x1 * cos - x2 * sin, x1 * sin + x2 * cos], axis=-1).astype(x.dtype)


def centred_depthwise_conv(x, w, b):
    s = x.shape[1]
    xp = jnp.pad(x, ((0, 0), (CONV_LEFT, CONV_WIDTH - 1 - CONV_LEFT), (0, 0)))
    out = b
    for k in range(CONV_WIDTH):
        out = out + xp[:, k:k + s] * w[k]
    return out


def block_diag_linear(x, w, b):
    xb = x.reshape(x.shape[:-1] + (LRU_HEADS, LRU_BLOCK))
    y = jnp.einsum('bshi,hij->bshj', xb, w) + b
    return y.reshape(x.shape)


def rglru_scan(x, w_a, b_a, w_x, b_x, lam, reverse):
    r = jax.nn.sigmoid(block_diag_linear(x, w_a, b_a).astype(jnp.float32))
    i = jax.nn.sigmoid(block_diag_linear(x, w_x, b_x).astype(jnp.float32))
    log_a = -LRU_C * r * jax.nn.softplus(-lam.astype(jnp.float32))
    a = jnp.exp(log_a)
    u = jnp.sqrt(-jnp.expm1(2.0 * log_a)) * (i * x.astype(jnp.float32))

    def combine(left, right):
        a1, b1 = left
        a2, b2 = right
        return a1 * a2, a2 * b1 + b2

    _, h = lax.associative_scan(combine, (a, u), reverse=reverse, axis=1)
    return h


def mla_attention(q_nope, q_pe, k_nope, k_pe, v):
    b, s, h, _ = q_nope.shape
    nb = s // Q_BLOCK
    scale = QK_HEAD ** -0.5

    def to_blocks(t):
        return jnp.moveaxis(t.reshape((b, nb, Q_BLOCK) + t.shape[2:]), 1, 0)

    def one_block(args):
        qn, qp = args
        sc = (jnp.einsum('bqhd,bkhd->bhqk', qn, k_nope, preferred_element_type=jnp.float32)
              + jnp.einsum('bqhr,bkr->bhqk', qp, k_pe, preferred_element_type=jnp.float32)) * scale
        p = jax.nn.softmax(sc, axis=-1).astype(v.dtype)
        return jnp.einsum('bhqk,bkhd->bqhd', p, v)

    o = lax.map(one_block, (to_blocks(q_nope), to_blocks(q_pe)))
    return jnp.moveaxis(o, 0, 1).reshape(b, s, h * V_HEAD)


def hybrid_mixer(xn, w_in, conv_w, conv_b, lru_wa, lru_ba, lru_wx, lru_bx, lru_lam,
                 q_norm, w_q_up, kv_norm, w_kv_up, w_lru_proj, w_mla_proj, w_out):
    b, s, _ = xn.shape
    dt = xn.dtype
    z = xn @ w_in
    x_lru, y_lru, c_q, c_kv, k_rope, gate_logits = jnp.split(z, IN_SPLITS, axis=-1)

    xc = centred_depthwise_conv(x_lru, conv_w, conv_b)
    h = (rglru_scan(xc, lru_wa[0], lru_ba[0], lru_wx[0], lru_bx[0], lru_lam[0], False)
         + rglru_scan(xc, lru_wa[1], lru_ba[1], lru_wx[1], lru_bx[1], lru_lam[1], True))
    o_lru = (h.astype(dt) * jax.nn.gelu(y_lru)) @ w_lru_proj

    cos, sin = rope_tables(s)
    q = (rmsnorm(c_q, q_norm) @ w_q_up).reshape(b, s, MLA_HEADS, QK_HEAD)
    q_nope = q[..., :QK_NOPE]
    q_pe = apply_rope(q[..., QK_NOPE:], cos[None, :, None, :], sin[None, :, None, :])
    kv = (rmsnorm(c_kv, kv_norm) @ w_kv_up).reshape(b, s, MLA_HEADS, QK_NOPE + V_HEAD)
    k_nope = kv[..., :QK_NOPE]
    v = kv[..., QK_NOPE:]
    k_pe = apply_rope(k_rope, cos[None], sin[None])
    o_mla = mla_attention(q_nope, q_pe, k_nope, k_pe, v) @ w_mla_proj

    g = jax.nn.sigmoid(gate_logits.astype(jnp.float32))
    g_a, g_b = g[..., :D_MODEL], g[..., D_MODEL:]
    merged = (g_a * o_lru.astype(jnp.float32) + g_b * o_mla.astype(jnp.float32)).astype(dt)
    return merged @ w_out


def trunk(x, norm1, w_in, conv_w, conv_b, lru_wa, lru_ba, lru_wx, lru_bx, lru_lam,
          q_norm, w_q_up, kv_norm, w_kv_up, w_lru_proj, w_mla_proj, w_out,
          norm2, w_up, w_down, norm_f):
    for l in range(DEPTH):
        x = x + hybrid_mixer(rmsnorm(x, norm1[l]), w_in[l], conv_w[l], conv_b[l],
                             lru_wa[l], lru_ba[l], lru_wx[l], lru_bx[l], lru_lam[l],
                             q_norm[l], w_q_up[l], kv_norm[l], w_kv_up[l],
                             w_lru_proj[l], w_mla_proj[l], w_out[l])
        u = jnp.square(jax.nn.relu(rmsnorm(x, norm2[l]) @ w_up[l]))
        x = x + u @ w_down[l]
    return rmsnorm(x, norm_f)


def setup_inputs(seed: int = 0) -> dict:
    key = jax.random.key(seed)
    ks = jax.random.split(key, 24)
    f32 = jnp.float32

    def nrm(k, shape, fan_in):
        return jax.random.normal(k, shape, f32) * (fan_in ** -0.5)

    def gain(k, shape):
        return 1.0 + 0.01 * jax.random.normal(k, shape, f32)

    def bias(k, shape):
        return 0.01 * jax.random.normal(k, shape, f32)

    u = jax.random.uniform(ks[10], (DEPTH, 2, LRU_WIDTH), f32, 0.9, 0.999)
    s = u ** (1.0 / LRU_C)
    lru_lam = jnp.log(s) - jnp.log1p(-s)

    return {
        "x_prompt": jax.random.normal(ks[0], (BATCH, SEQ, D_MODEL), f32),
        "x_sample": jax.random.normal(ks[1], (DEC_BATCH, DEC_SEQ, D_MODEL), f32),
        "norm1": gain(ks[2], (DEPTH, D_MODEL)),
        "w_in": nrm(ks[3], (DEPTH, D_MODEL, IN_COLS), D_MODEL),
        "conv_w": nrm(ks[4], (DEPTH, CONV_WIDTH, LRU_WIDTH), CONV_WIDTH),
        "conv_b": bias(ks[5], (DEPTH, LRU_WIDTH)),
        "lru_wa": nrm(ks[6], (DEPTH, 2, LRU_HEADS, LRU_BLOCK, LRU_BLOCK), LRU_BLOCK),
        "lru_ba": bias(ks[7], (DEPTH, 2, LRU_HEADS, LRU_BLOCK)),
        "lru_wx": nrm(ks[8], (DEPTH, 2, LRU_HEADS, LRU_BLOCK, LRU_BLOCK), LRU_BLOCK),
        "lru_bx": bias(ks[9], (DEPTH, 2, LRU_HEADS, LRU_BLOCK)),
        "lru_lam": lru_lam,
        "q_norm": gain(ks[11], (DEPTH, Q_LORA)),
        "w_q_up": nrm(ks[12], (DEPTH, Q_LORA, MLA_HEADS * QK_HEAD), Q_LORA),
        "kv_norm": gain(ks[13], (DEPTH, KV_LORA)),
        "w_kv_up": nrm(ks[14], (DEPTH, KV_LORA, MLA_HEADS * (QK_NOPE + V_HEAD)), KV_LORA),
        "w_lru_proj": nrm(ks[15], (DEPTH, LRU_WIDTH, D_MODEL), LRU_WIDTH),
        "w_mla_proj": nrm(ks[16], (DEPTH, MLA_WIDTH, D_MODEL), MLA_WIDTH),
        "w_out": nrm(ks[17], (DEPTH, D_MODEL, D_MODEL), D_MODEL),
        "norm2": gain(ks[18], (DEPTH, D_MODEL)),
        "w_up": nrm(ks[19], (DEPTH, D_MODEL, D_FF), D_MODEL),
        "w_down": nrm(ks[20], (DEPTH, D_FF, D_MODEL), D_FF),
        "norm_f": gain(ks[21], (D_MODEL,)),
    }


def reference(x_prompt, x_sample, norm1, w_in, conv_w, conv_b, lru_wa, lru_ba, lru_wx, lru_bx,
              lru_lam, q_norm, w_q_up, kv_norm, w_kv_up, w_lru_proj, w_mla_proj, w_out,
              norm2, w_up, w_down, norm_f):
    y_prompt = trunk(x_prompt, norm1, w_in, conv_w, conv_b, lru_wa, lru_ba, lru_wx, lru_bx,
                     lru_lam, q_norm, w_q_up, kv_norm, w_kv_up, w_lru_proj, w_mla_proj, w_out,
                     norm2, w_up, w_down, norm_f)
    y_sample = trunk(x_sample, norm1, w_in, conv_w, conv_b, lru_wa, lru_ba, lru_wx, lru_bx,
                     lru_lam, q_norm, w_q_up, kv_norm, w_kv_up, w_lru_proj, w_mla_proj, w_out,
                     norm2, w_up, w_down, norm_f)
    return (y_prompt, y_sample)
```

```python
import functools

import jax
import jax.numpy as jnp
import numpy as np
from jax import lax
from jax.experimental import pallas as pl
from jax.experimental.pallas import tpu as pltpu

D_MODEL = 4096
LRU_HEADS = 16
LRU_BLOCK = 256
CONV_WIDTH = 4
CONV_LEFT = 2
LRU_C = 8.0
MLA_HEADS = 32
Q_LORA = 1024
KV_LORA = 512
QK_NOPE = 128
QK_ROPE = 64
V_HEAD = 128
QK_HEAD = QK_NOPE + QK_ROPE
ROPE_THETA = 10000.0
D_FF = 4 * D_MODEL
EPS = 1e-6

LANES = 128
SUBLANES = 8
QK_PAD = 2 * LANES
SMALL_COLS = 13 * LANES
VMEM_CAP = 60 * 1024 * 1024

BF16 = jnp.bfloat16
F32 = jnp.float32


def _cparams(semantics, vmem_bytes):
    return pltpu.CompilerParams(dimension_semantics=semantics,
                                vmem_limit_bytes=min(int(vmem_bytes), VMEM_CAP))


def _dot(a, b):
    return jnp.dot(a, b, preferred_element_type=F32)


def _rms(x, g):
    y = x * lax.rsqrt(jnp.mean(jnp.square(x), axis=-1, keepdims=True) + EPS)
    return y * g


def _rmsnorm_kernel(x_ref, g_ref, o_ref):
    o_ref[...] = _rms(x_ref[...], g_ref[...]).astype(o_ref.dtype)


def rmsnorm(x, g, out_dtype, tr=256):
    t, d = x.shape
    return pl.pallas_call(
        _rmsnorm_kernel,
        grid=(t // tr,),
        in_specs=[pl.BlockSpec((tr, d), lambda i: (i, 0)),
                  pl.BlockSpec((1, d), lambda i: (0, 0))],
        out_specs=pl.BlockSpec((tr, d), lambda i: (i, 0)),
        out_shape=jax.ShapeDtypeStruct((t, d), out_dtype),
        compiler_params=_cparams(("parallel",), 40 << 20),
        name="rmsnorm",
    )(x, g.reshape(1, d))


def _mm_kernel(a_ref, b_ref, o_ref):
    o_ref[...] = _dot(a_ref[...], b_ref[...]).astype(o_ref.dtype)


def matmul(a, b, out_dtype, tm=1024, tn=1024, name="matmul"):
    m, k = a.shape
    n = b.shape[1]
    return pl.pallas_call(
        _mm_kernel,
        grid=(m // tm, n // tn),
        in_specs=[pl.BlockSpec((tm, k), lambda i, j: (i, 0)),
                  pl.BlockSpec((k, tn), lambda i, j: (0, j))],
        out_specs=pl.BlockSpec((tm, tn), lambda i, j: (i, j)),
        out_shape=jax.ShapeDtypeStruct((m, n), out_dtype),
        compiler_params=_cparams(("parallel", "parallel"), 56 << 20),
        name=name,
    )(a, b)


def _rope128(x, c, s):
    lane = lax.broadcasted_iota(jnp.int32, x.shape, 1)
    swapped = jnp.where(lane < QK_ROPE // 2,
                        pltpu.roll(x, LANES - QK_ROPE // 2, 1),
                        pltpu.roll(x, QK_ROPE // 2, 1))
    return x * c + swapped * s


def _small_proj_kernel(a_ref, w_ref, qn_ref, kvn_ref, c_ref, s_ref, cq_ref, ckv_ref, kpe_ref):
    z = _dot(a_ref[...], w_ref[...])
    cq_ref[...] = _rms(z[:, :Q_LORA], qn_ref[...]).astype(cq_ref.dtype)
    ckv_ref[...] = _rms(z[:, Q_LORA:Q_LORA + KV_LORA], kvn_ref[...]).astype(ckv_ref.dtype)
    kr = z[:, Q_LORA + KV_LORA:]
    kpe_ref[...] = _rope128(kr, c_ref[...], s_ref[...]).astype(kpe_ref.dtype)


def small_proj(xn, w_small, q_norm, kv_norm, rope_c, rope_s, seq, tm=512):
    t, d = xn.shape
    nseq = seq // tm
    return pl.pallas_call(
        _small_proj_kernel,
        grid=(t // tm,),
        in_specs=[pl.BlockSpec((tm, d), lambda i: (i, 0)),
                  pl.BlockSpec((d, SMALL_COLS), lambda i: (0, 0)),
                  pl.BlockSpec((1, Q_LORA), lambda i: (0, 0)),
                  pl.BlockSpec((1, KV_LORA), lambda i: (0, 0)),
                  pl.BlockSpec((tm, LANES), lambda i: (i % nseq, 0)),
                  pl.BlockSpec((tm, LANES), lambda i: (i % nseq, 0))],
        out_specs=[pl.BlockSpec((tm, Q_LORA), lambda i: (i, 0)),
                   pl.BlockSpec((tm, KV_LORA), lambda i: (i, 0)),
                   pl.BlockSpec((tm, LANES), lambda i: (i, 0))],
        out_shape=[jax.ShapeDtypeStruct((t, Q_LORA), BF16),
                   jax.ShapeDtypeStruct((t, KV_LORA), BF16),
                   jax.ShapeDtypeStruct((t, LANES), BF16)],
        compiler_params=_cparams(("parallel",), 56 << 20),
        name="small_proj",
    )(xn, w_small, q_norm.reshape(1, -1), kv_norm.reshape(1, -1), rope_c, rope_s)


def _softplus(x):
    return jnp.maximum(x, 0.0) + jnp.log1p(jnp.exp(-jnp.abs(x)))


def _scan_chunk(a, u, h0, reverse):
    tc, w = a.shape
    n = tc // SUBLANES
    a3 = a.reshape(n, SUBLANES, w)
    u3 = u.reshape(n, SUBLANES, w)
    sub = lax.broadcasted_iota(jnp.int32, (n, SUBLANES, w), 1)
    for k in (1, 2, 4):
        if reverse:
            shift, mask = SUBLANES - k, sub < SUBLANES - k
        else:
            shift, mask = k, sub >= k
        a_sh = pltpu.roll(a3, shift, 1)
        u_sh = pltpu.roll(u3, shift, 1)
        u3 = jnp.where(mask, a3 * u_sh + u3, u3)
        a3 = jnp.where(mask, a3 * a_sh, a3)
    outs = [None] * n
    h = h0
    for j in (range(n - 1, -1, -1) if reverse else range(n)):
        hj = u3[j] + a3[j] * h
        h = hj[0:1] if reverse else hj[SUBLANES - 1:SUBLANES]
        outs[j] = hj
    return jnp.concatenate(outs, axis=0), h


def _lru_kernel(x_ref, prev_ref, next_ref, y_ref, cw_ref, cb_ref, wa_ref, ba_ref, wx_ref, bx_ref,
                lam_ref, o_ref, hf_ref, carry_ref, *, n_chunks, tc):
    s = pl.program_id(2)
    backward = s >= n_chunks
    tidx = jnp.where(backward, 2 * n_chunks - 1 - s, s)

    x = x_ref[...]
    prev = jnp.where(tidx == 0, 0.0, prev_ref[...])
    nxt = jnp.where(tidx == n_chunks - 1, 0.0, next_ref[...])
    xe = jnp.concatenate([prev, x, nxt], axis=0)
    cw = cw_ref[...]
    xc = cb_ref[...]
    for k in range(CONV_WIDTH):
        off = SUBLANES - CONV_LEFT + k
        xc = xc + xe[off:off + tc] * cw[k:k + 1]

    xcb = xc.astype(BF16)
    r = jax.nn.sigmoid(_dot(xcb, wa_ref[...]) + ba_ref[...])
    i = jax.nn.sigmoid(_dot(xcb, wx_ref[...]) + bx_ref[...])
    log_a = -LRU_C * r * _softplus(-lam_ref[...])
    a = jnp.exp(log_a)
    u = jnp.sqrt(1.0 - a * a) * (i * xc)

    @pl.when(jnp.logical_or(s == 0, s == n_chunks))
    def _():
        carry_ref[...] = jnp.zeros_like(carry_ref)

    row0 = pl.multiple_of(tidx * tc, tc)

    @pl.when(jnp.logical_not(backward))
    def _():
        h, last = _scan_chunk(a, u, carry_ref[0:1], reverse=False)
        hf_ref[pl.ds(row0, tc), :] = h
        carry_ref[0:1] = last

    @pl.when(backward)
    def _():
        h, last = _scan_chunk(a, u, carry_ref[0:1], reverse=True)
        carry_ref[0:1] = last
        tot = hf_ref[pl.ds(row0, tc), :] + h
        o_ref[...] = (tot * jax.nn.gelu(y_ref[...])).astype(o_ref.dtype)


def lru_branch(z1, conv_w, conv_b, wa, ba, wx, bx, lam, batch, seq, tc=256):
    t = z1.shape[0]
    w = LRU_BLOCK
    n_chunks = seq // tc
    y_col0 = D_MODEL // w

    def tix(s):
        return jnp.where(s >= n_chunks, 2 * n_chunks - 1 - s, s)

    def x_map(b, h, s):
        return (b * n_chunks + tix(s), h)

    def prev_map(b, h, s):
        return (jnp.maximum((b * seq + tix(s) * tc) // SUBLANES - 1, 0), h)

    def next_map(b, h, s):
        return (jnp.minimum((b * seq + (tix(s) + 1) * tc) // SUBLANES, t // SUBLANES - 1), h)

    def y_map(b, h, s):
        return (b * n_chunks + jnp.where(s >= n_chunks, 2 * n_chunks - 1 - s, n_chunks - 1), y_col0 + h)

    def o_map(b, h, s):
        return (b * n_chunks + jnp.where(s >= n_chunks, 2 * n_chunks - 1 - s, n_chunks - 1), h)

    def dir_map(b, h, s):
        return (s // n_chunks, h, 0, 0)

    return pl.pallas_call(
        functools.partial(_lru_kernel, n_chunks=n_chunks, tc=tc),
        grid=(batch, LRU_HEADS, 2 * n_chunks),
        in_specs=[pl.BlockSpec((tc, w), x_map),
                  pl.BlockSpec((SUBLANES, w), prev_map),
                  pl.BlockSpec((SUBLANES, w), next_map),
                  pl.BlockSpec((tc, w), y_map),
                  pl.BlockSpec((CONV_WIDTH, w), lambda b, h, s: (0, h)),
                  pl.BlockSpec((1, w), lambda b, h, s: (0, h)),
                  pl.BlockSpec((None, None, w, w), dir_map),
                  pl.BlockSpec((None, None, 1, w), dir_map),
                  pl.BlockSpec((None, None, w, w), dir_map),
                  pl.BlockSpec((None, None, 1, w), dir_map),
                  pl.BlockSpec((None, None, 1, w), dir_map)],
        out_specs=pl.BlockSpec((tc, w), o_map),
        out_shape=jax.ShapeDtypeStruct((t, D_MODEL), BF16),
        scratch_shapes=[pltpu.VMEM((seq, w), F32), pltpu.VMEM((SUBLANES, w), F32)],
        compiler_params=_cparams(("parallel", "parallel", "arbitrary"), 40 << 20),
        name="lru_branch",
    )(z1, z1, z1, z1, conv_w, conv_b.reshape(1, -1), wa, ba, wx, bx, lam)


def _q_up_kernel(a_ref, w_ref, c_ref, s_ref, o_ref, *, heads):
    q = _dot(a_ref[...], w_ref[...])
    c = c_ref[...]
    s = s_ref[...]
    for h in range(heads):
        lo = h * QK_PAD
        o_ref[:, lo:lo + LANES] = q[:, lo:lo + LANES].astype(o_ref.dtype)
        o_ref[:, lo + LANES:lo + QK_PAD] = _rope128(q[:, lo + LANES:lo + QK_PAD], c, s).astype(o_ref.dtype)


def q_up(cq, w_q, rope_c, rope_s, seq, tm=512, heads=4):
    t, k = cq.shape
    n = w_q.shape[1]
    tn = heads * QK_PAD
    nseq = seq // tm
    return pl.pallas_call(
        functools.partial(_q_up_kernel, heads=heads),
        grid=(t // tm, n // tn),
        in_specs=[pl.BlockSpec((tm, k), lambda i, j: (i, 0)),
                  pl.BlockSpec((k, tn), lambda i, j: (0, j)),
                  pl.BlockSpec((tm, LANES), lambda i, j: (i % nseq, 0)),
                  pl.BlockSpec((tm, LANES), lambda i, j: (i % nseq, 0))],
        out_specs=pl.BlockSpec((tm, tn), lambda i, j: (i, j)),
        out_shape=jax.ShapeDtypeStruct((t, n), BF16),
        compiler_params=_cparams(("parallel", "parallel"), 40 << 20),
        name="q_up",
    )(cq, w_q, rope_c, rope_s)


def _kv_up_kernel(a_ref, wk_ref, wv_ref, kpe_ref, k_ref, v_ref, *, heads):
    a = a_ref[...]
    kn = _dot(a, wk_ref[...])
    kpe = kpe_ref[...]
    for h in range(heads):
        k_ref[:, h * QK_PAD:h * QK_PAD + LANES] = kn[:, h * LANES:(h + 1) * LANES].astype(k_ref.dtype)
        k_ref[:, h * QK_PAD + LANES:(h + 1) * QK_PAD] = kpe
    v_ref[...] = _dot(a, wv_ref[...]).astype(v_ref.dtype)


def kv_up(ckv, w_k, w_v, kpe, tm=512, heads=8):
    t, k = ckv.shape
    n = w_k.shape[1]
    tn = heads * LANES
    return pl.pallas_call(
        functools.partial(_kv_up_kernel, heads=heads),
        grid=(t // tm, n // tn),
        in_specs=[pl.BlockSpec((tm, k), lambda i, j: (i, 0)),
                  pl.BlockSpec((k, tn), lambda i, j: (0, j)),
                  pl.BlockSpec((k, tn), lambda i, j: (0, j)),
                  pl.BlockSpec((tm, LANES), lambda i, j: (i, 0))],
        out_specs=[pl.BlockSpec((tm, heads * QK_PAD), lambda i, j: (i, j)),
                   pl.BlockSpec((tm, tn), lambda i, j: (i, j))],
        out_shape=[jax.ShapeDtypeStruct((t, MLA_HEADS * QK_PAD), BF16),
                   jax.ShapeDtypeStruct((t, n), BF16)],
        compiler_params=_cparams(("parallel", "parallel"), 40 << 20),
        name="kv_up",
    )(ckv, w_k, w_v, kpe)


def _attn_kernel(q_ref, k_ref, v_ref, o_ref):
    sc = lax.dot_general(q_ref[...], k_ref[...], (((1,), (1,)), ((), ())),
                         preferred_element_type=F32)
    m = jnp.max(sc, axis=-1, keepdims=True)
    p = jnp.exp((sc - m) * (QK_HEAD ** -0.5))
    l = jnp.sum(p, axis=-1, keepdims=True)
    o = _dot(p.astype(BF16), v_ref[...])
    o_ref[...] = (o / l).astype(o_ref.dtype)


def attention(q, k, v, batch, seq, tq):
    t = q.shape[0]
    nq = seq // tq
    return pl.pallas_call(
        _attn_kernel,
        grid=(batch, MLA_HEADS, nq),
        in_specs=[pl.BlockSpec((tq, QK_PAD), lambda b, h, i: (b * nq + i, h)),
                  pl.BlockSpec((seq, QK_PAD), lambda b, h, i: (b, h)),
                  pl.BlockSpec((seq, V_HEAD), lambda b, h, i: (b, h))],
        out_specs=pl.BlockSpec((tq, V_HEAD), lambda b, h, i: (b * nq + i, h)),
        out_shape=jax.ShapeDtypeStruct((t, MLA_HEADS * V_HEAD), BF16),
        compiler_params=_cparams(("parallel", "parallel", "arbitrary"), 56 << 20),
        name="attention",
    )(q, k, v)


def _merge_kernel(a_ref, w_ref, ol_ref, ga_ref, gb_ref, o_ref):
    om = _dot(a_ref[...], w_ref[...])
    merged = jax.nn.sigmoid(ga_ref[...]) * ol_ref[...] + jax.nn.sigmoid(gb_ref[...]) * om
    o_ref[...] = merged.astype(o_ref.dtype)


def merge_proj(attn, w_mla, o_lru, z1, tm=512, tn=1024):
    m, k = attn.shape
    n = w_mla.shape[1]
    ga0 = 2 * D_MODEL // tn
    gb0 = 3 * D_MODEL // tn
    return pl.pallas_call(
        _merge_kernel,
        grid=(m // tm, n // tn),
        in_specs=[pl.BlockSpec((tm, k), lambda i, j: (i, 0)),
                  pl.BlockSpec((k, tn), lambda i, j: (0, j)),
                  pl.BlockSpec((tm, tn), lambda i, j: (i, j)),
                  pl.BlockSpec((tm, tn), lambda i, j: (i, ga0 + j)),
                  pl.BlockSpec((tm, tn), lambda i, j: (i, gb0 + j))],
        out_specs=pl.BlockSpec((tm, tn), lambda i, j: (i, j)),
        out_shape=jax.ShapeDtypeStruct((m, n), BF16),
        compiler_params=_cparams(("parallel", "parallel"), 56 << 20),
        name="merge_proj",
    )(attn, w_mla, o_lru, z1, z1)


def _mm_res_kernel(a_ref, w_ref, r_ref, o_ref):
    o_ref[...] = r_ref[...] + _dot(a_ref[...], w_ref[...])


def matmul_residual(a, w, res, tm=512, tn=1024):
    m, k = a.shape
    n = w.shape[1]
    return pl.pallas_call(
        _mm_res_kernel,
        grid=(m // tm, n // tn),
        in_specs=[pl.BlockSpec((tm, k), lambda i, j: (i, 0)),
                  pl.BlockSpec((k, tn), lambda i, j: (0, j)),
                  pl.BlockSpec((tm, tn), lambda i, j: (i, j))],
        out_specs=pl.BlockSpec((tm, tn), lambda i, j: (i, j)),
        out_shape=jax.ShapeDtypeStruct((m, n), F32),
        compiler_params=_cparams(("parallel", "parallel"), 56 << 20),
        name="matmul_residual",
    )(a, w, res)


def _up_kernel(a_ref, w_ref, o_ref):
    o_ref[...] = jnp.square(jnp.maximum(_dot(a_ref[...], w_ref[...]), 0.0)).astype(o_ref.dtype)


def mlp_up(a, w, tm=1024, tn=1024):
    m, k = a.shape
    n = w.shape[1]
    return pl.pallas_call(
        _up_kernel,
        grid=(m // tm, n // tn),
        in_specs=[pl.BlockSpec((tm, k), lambda i, j: (i, 0)),
                  pl.BlockSpec((k, tn), lambda i, j: (0, j))],
        out_specs=pl.BlockSpec((tm, tn), lambda i, j: (i, j)),
        out_shape=jax.ShapeDtypeStruct((m, n), BF16),
        compiler_params=_cparams(("parallel", "parallel"), 56 << 20),
        name="mlp_up",
    )(a, w)


def _down_kernel(a_ref, w_ref, r_ref, o_ref, acc_ref):
    kk = pl.program_id(2)

    @pl.when(kk == 0)
    def _():
        acc_ref[...] = r_ref[...]

    acc_ref[...] += _dot(a_ref[...], w_ref[...])

    @pl.when(kk == pl.num_programs(2) - 1)
    def _():
        o_ref[...] = acc_ref[...]


def mlp_down(a, w, res, tm=1024, tn=1024, tk=2048):
    m, k = a.shape
    n = w.shape[1]
    return pl.pallas_call(
        _down_kernel,
        grid=(m // tm, n // tn, k // tk),
        in_specs=[pl.BlockSpec((tm, tk), lambda i, j, kk: (i, kk)),
                  pl.BlockSpec((tk, tn), lambda i, j, kk: (kk, j)),
                  pl.BlockSpec((tm, tn), lambda i, j, kk: (i, j))],
        out_specs=pl.BlockSpec((tm, tn), lambda i, j, kk: (i, j)),
        out_shape=jax.ShapeDtypeStruct((m, n), F32),
        scratch_shapes=[pltpu.VMEM((tm, tn), F32)],
        compiler_params=_cparams(("parallel", "parallel", "arbitrary"), 56 << 20),
        name="mlp_down",
    )(a, w, res)


def _rope_tables(seq):
    inv = 1.0 / (ROPE_THETA ** (jnp.arange(0, QK_ROPE, 2, dtype=F32) / QK_ROPE))
    ang = jnp.arange(seq, dtype=F32)[:, None] * inv[None, :]
    cos, sin = jnp.cos(ang), jnp.sin(ang)
    zero = jnp.zeros((seq, LANES - QK_ROPE), F32)
    return (jnp.concatenate([cos, cos, zero], axis=1),
            jnp.concatenate([-sin, sin, zero], axis=1))


def _prep_weights(w_in, lru_wa, lru_ba, lru_wx, lru_bx, lru_lam, w_q_up, w_kv_up,
                  w_lru_proj, w_mla_proj, w_out, w_up, w_down):
    d = D_MODEL
    w_main = jnp.concatenate([w_in[:, :2 * d], w_in[:, 2 * d + Q_LORA + KV_LORA + QK_ROPE:]],
                             axis=1).astype(BF16)
    small = w_in[:, 2 * d:2 * d + Q_LORA + KV_LORA + QK_ROPE]
    w_small = jnp.pad(small, ((0, 0), (0, SMALL_COLS - small.shape[1]))).astype(BF16)
    wq = w_q_up.reshape(Q_LORA, MLA_HEADS, QK_HEAD)
    wq = jnp.pad(wq, ((0, 0), (0, 0), (0, QK_PAD - QK_HEAD))).reshape(Q_LORA, MLA_HEADS * QK_PAD)
    wkv = w_kv_up.reshape(KV_LORA, MLA_HEADS, QK_NOPE + V_HEAD)
    wk = wkv[:, :, :QK_NOPE].reshape(KV_LORA, MLA_HEADS * QK_NOPE)
    wv = wkv[:, :, QK_NOPE:].reshape(KV_LORA, MLA_HEADS * V_HEAD)
    shp = (2, LRU_HEADS, 1, LRU_BLOCK)
    return dict(
        w_main=w_main, w_small=w_small, wq=wq.astype(BF16), wk=wk.astype(BF16), wv=wv.astype(BF16),
        wa=lru_wa.astype(BF16), wx=lru_wx.astype(BF16),
        ba=lru_ba.reshape(shp), bx=lru_bx.reshape(shp), lam=lru_lam.reshape(shp),
        w_lru_proj=w_lru_proj.astype(BF16), w_mla_proj=w_mla_proj.astype(BF16),
        w_out=w_out.astype(BF16), w_up=w_up.astype(BF16), w_down=w_down.astype(BF16))


def _layer(x3, p, norm1, conv_w, conv_b, q_norm, kv_norm, norm2, norm_f):
    batch, seq, d = x3.shape
    x = x3.reshape(batch * seq, d)
    rope_c, rope_s = _rope_tables(seq)

    xn = rmsnorm(x, norm1, BF16)
    z1 = matmul(xn, p["w_main"], F32, name="in_proj")
    cq, ckv, kpe = small_proj(xn, p["w_small"], q_norm, kv_norm, rope_c, rope_s, seq)

    a_lru = lru_branch(z1, conv_w, conv_b, p["wa"], p["ba"], p["wx"], p["bx"], p["lam"], batch, seq)
    o_lru = matmul(a_lru, p["w_lru_proj"], F32, name="lru_proj")

    q = q_up(cq, p["wq"], rope_c, rope_s, seq)
    k, v = kv_up(ckv, p["wk"], p["wv"], kpe)
    attn = attention(q, k, v, batch, seq, tq=256 if seq > 4096 else 512)

    merged = merge_proj(attn, p["w_mla_proj"], o_lru, z1)
    h = matmul_residual(merged, p["w_out"], x)

    hn = rmsnorm(h, norm2, BF16)
    u = mlp_up(hn, p["w_up"])
    h2 = mlp_down(u, p["w_down"], h)
    return rmsnorm(h2, norm_f, F32).reshape(batch, seq, d)


def kernel(x_prompt, x_sample, norm1, w_in, conv_w, conv_b, lru_wa, lru_ba, lru_wx, lru_bx, lru_lam,
           q_norm, w_q_up, kv_norm, w_kv_up, w_lru_proj, w_mla_proj, w_out, norm2, w_up, w_down, norm_f):
    assert norm1.shape[0] == 1, "single-layer trunk"
    p = _prep_weights(w_in[0], lru_wa[0], lru_ba[0], lru_wx[0], lru_bx[0], lru_lam[0], w_q_up[0],
                      w_kv_up[0], w_lru_proj[0], w_mla_proj[0], w_out[0], w_up[0], w_down[0])
    args = (p, norm1[0], conv_w[0], conv_b[0], q_norm[0], kv_norm[0], norm2[0], norm_f)
    return (_layer(x_prompt, *args), _layer(x_sample, *args))
```

```python
import functools
import math

import jax
import jax.numpy as jnp
from jax import lax
from jax.experimental import pallas as pl
from jax.experimental.pallas import tpu as pltpu

D_MODEL = 4096
LRU_HEADS = 16
LRU_BLOCK = 256
CONV_WIDTH = 4
CONV_LEFT = 2
LRU_C = 8.0
MLA_HEADS = 32
Q_LORA = 1024
KV_LORA = 512
QK_NOPE = 128
QK_ROPE = 64
V_HEAD = 128
QK_HEAD = QK_NOPE + QK_ROPE
ROPE_THETA = 10000.0
D_FF = 4 * D_MODEL
EPS = 1e-6
LOG2E = math.log2(math.e)

LANES = 128
SUBLANES = 8
QK_PAD = 2 * LANES
V_PAD = 2 * LANES
SMALL_COLS = 13 * LANES
VMEM_CAP = 60 * 1024 * 1024

BF16 = jnp.bfloat16
F32 = jnp.float32


def _cparams(semantics, vmem_bytes):
    return pltpu.CompilerParams(dimension_semantics=semantics,
                                vmem_limit_bytes=min(int(vmem_bytes), VMEM_CAP))


def _dot(a, b):
    return jnp.dot(a, b, preferred_element_type=F32)


def _rms(x, g):
    y = x * lax.rsqrt(jnp.mean(jnp.square(x), axis=-1, keepdims=True) + EPS)
    return y * g


def _rmsnorm_kernel(x_ref, g_ref, o_ref):
    o_ref[...] = _rms(x_ref[...], g_ref[...]).astype(o_ref.dtype)


def rmsnorm(x, g, out_dtype, tr=256):
    t, d = x.shape
    return pl.pallas_call(
        _rmsnorm_kernel,
        grid=(t // tr,),
        in_specs=[pl.BlockSpec((tr, d), lambda i: (i, 0)),
                  pl.BlockSpec((1, d), lambda i: (0, 0))],
        out_specs=pl.BlockSpec((tr, d), lambda i: (i, 0)),
        out_shape=jax.ShapeDtypeStruct((t, d), out_dtype),
        compiler_params=_cparams(("parallel",), 40 << 20),
        name="rmsnorm",
    )(x, g.reshape(1, d))


def _mm_kernel(a_ref, b_ref, o_ref):
    o_ref[...] = _dot(a_ref[...], b_ref[...]).astype(o_ref.dtype)


def matmul(a, b, out_dtype, tm=1024, tn=1024, name="matmul"):
    m, k = a.shape
    n = b.shape[1]
    return pl.pallas_call(
        _mm_kernel,
        grid=(m // tm, n // tn),
        in_specs=[pl.BlockSpec((tm, k), lambda i, j: (i, 0)),
                  pl.BlockSpec((k, tn), lambda i, j: (0, j))],
        out_specs=pl.BlockSpec((tm, tn), lambda i, j: (i, j)),
        out_shape=jax.ShapeDtypeStruct((m, n), out_dtype),
        compiler_params=_cparams(("parallel", "parallel"), 56 << 20),
        name=name,
    )(a, b)


def _rope128(x, c, s):
    lane = lax.broadcasted_iota(jnp.int32, x.shape, 1)
    swapped = jnp.where(lane < QK_ROPE // 2,
                        pltpu.roll(x, LANES - QK_ROPE // 2, 1),
                        pltpu.roll(x, QK_ROPE // 2, 1))
    return x * c + swapped * s


def _small_proj_kernel(a_ref, w_ref, qn_ref, kvn_ref, c_ref, s_ref, cq_ref, ckv_ref, kpe_ref):
    z = _dot(a_ref[...], w_ref[...])
    cq_ref[...] = _rms(z[:, :Q_LORA], qn_ref[...]).astype(cq_ref.dtype)
    ckv_ref[...] = _rms(z[:, Q_LORA:Q_LORA + KV_LORA], kvn_ref[...]).astype(ckv_ref.dtype)
    kr = z[:, Q_LORA + KV_LORA:]
    kpe_ref[...] = _rope128(kr, c_ref[...], s_ref[...]).astype(kpe_ref.dtype)


def small_proj(xn, w_small, q_norm, kv_norm, rope_c, rope_s, seq, tm=512):
    t, d = xn.shape
    nseq = seq // tm
    return pl.pallas_call(
        _small_proj_kernel,
        grid=(t // tm,),
        in_specs=[pl.BlockSpec((tm, d), lambda i: (i, 0)),
                  pl.BlockSpec((d, SMALL_COLS), lambda i: (0, 0)),
                  pl.BlockSpec((1, Q_LORA), lambda i: (0, 0)),
                  pl.BlockSpec((1, KV_LORA), lambda i: (0, 0)),
                  pl.BlockSpec((tm, LANES), lambda i: (i % nseq, 0)),
                  pl.BlockSpec((tm, LANES), lambda i: (i % nseq, 0))],
        out_specs=[pl.BlockSpec((tm, Q_LORA), lambda i: (i, 0)),
                   pl.BlockSpec((tm, KV_LORA), lambda i: (i, 0)),
                   pl.BlockSpec((tm, LANES), lambda i: (i, 0))],
        out_shape=[jax.ShapeDtypeStruct((t, Q_LORA), BF16),
                   jax.ShapeDtypeStruct((t, KV_LORA), BF16),
                   jax.ShapeDtypeStruct((t, LANES), BF16)],
        compiler_params=_cparams(("parallel",), 56 << 20),
        name="small_proj",
    )(xn, w_small, q_norm.reshape(1, -1), kv_norm.reshape(1, -1), rope_c, rope_s)


def _softplus(x):
    return jnp.maximum(x, 0.0) + jnp.log1p(jnp.exp(-jnp.abs(x)))


def _segment_rows(tc):
    n = tc // SUBLANES
    return [j0 * SUBLANES + i for i, j0 in (divmod(m * SUBLANES, n) for m in range(n))]


def _to_segment_major(buf_ref, slot0, src_ref):
    tc, w = src_ref.shape
    for c in range(w // LANES):
        for m, row in enumerate(_segment_rows(tc)):
            buf_ref.at[c][pl.ds(slot0 * SUBLANES + row, SUBLANES, stride=SUBLANES), :] = (
                src_ref[m * SUBLANES:(m + 1) * SUBLANES, c * LANES:(c + 1) * LANES])


def _from_segment_major(buf_ref):
    tc = buf_ref.shape[1]
    return jnp.concatenate(
        [jnp.concatenate([buf_ref.at[c][pl.ds(row, SUBLANES, stride=SUBLANES), :]
                          for row in _segment_rows(tc)], axis=0)
         for c in range(buf_ref.shape[0])], axis=1)


def _conv_segment_major(xbuf_ref, x_ref, prev, nxt, cw, cb):
    tc = x_ref.shape[0]
    n = tc // SUBLANES
    _to_segment_major(xbuf_ref, CONV_LEFT, x_ref)
    sub = lax.broadcasted_iota(jnp.int32, (SUBLANES, LANES), 0)
    out = []
    for c in range(xbuf_ref.shape[0]):
        lanes = slice(c * LANES, (c + 1) * LANES)

        def slot(j):
            return xbuf_ref[c, (j + CONV_LEFT) * SUBLANES:(j + CONV_LEFT + 1) * SUBLANES, :]

        for back in range(1, CONV_LEFT + 1):
            halo = jnp.broadcast_to(prev[SUBLANES - back:SUBLANES - back + 1, lanes], sub.shape)
            xbuf_ref[c, (CONV_LEFT - back) * SUBLANES:(CONV_LEFT - back + 1) * SUBLANES, :] = jnp.where(
                sub == 0, halo, pltpu.roll(slot(n - back), 1, 0))
        for fwd in range(CONV_WIDTH - 1 - CONV_LEFT):
            halo = jnp.broadcast_to(nxt[fwd:fwd + 1, lanes], sub.shape)
            xbuf_ref[c, (n + CONV_LEFT + fwd) * SUBLANES:(n + CONV_LEFT + fwd + 1) * SUBLANES, :] = jnp.where(
                sub == SUBLANES - 1, halo, pltpu.roll(slot(fwd), SUBLANES - 1, 0))
        xc = cb[:, lanes]
        for k in range(CONV_WIDTH):
            xc = xc + xbuf_ref[c, k * SUBLANES:k * SUBLANES + tc, :] * cw[k:k + 1, lanes]
        out.append(xc)
    return jnp.concatenate(out, axis=1)


def _scan_segments(a, u, c_in, reverse):
    n = a.shape[0] // SUBLANES
    local = [None] * n
    prod = [None] * n
    h = p = None
    for j in (range(n - 1, -1, -1) if reverse else range(n)):
        aj = a[j * SUBLANES:(j + 1) * SUBLANES]
        uj = u[j * SUBLANES:(j + 1) * SUBLANES]
        if h is None:
            h, p = uj, aj
        else:
            h, p = aj * h + uj, aj * p
        local[j], prod[j] = h, p
    carries = [None] * SUBLANES
    c = c_in
    for i in (range(SUBLANES - 1, -1, -1) if reverse else range(SUBLANES)):
        carries[i] = c
        c = h[i:i + 1] + p[i:i + 1] * c
    cmat = jnp.concatenate(carries, axis=0)
    out = jnp.concatenate([local[j] + prod[j] * cmat for j in range(n)], axis=0)
    return out, c


def _lru_kernel(x_ref, prev_ref, next_ref, y_ref, cw_ref, cb_ref, wa_ref, ba_ref, wx_ref, bx_ref,
                lam_ref, o_ref, xbuf_ref, obuf_ref, xc_ref, hf_ref, carry_ref, *, n_chunks, tc):
    s = pl.program_id(2)
    rate = (-LRU_C * LOG2E) * _softplus(-lam_ref[...])

    def gates_scan(xc, reverse):
        xcb = xc.astype(BF16)
        r = jax.nn.sigmoid(_dot(xcb, wa_ref[...]) + ba_ref[...])
        i = jax.nn.sigmoid(_dot(xcb, wx_ref[...]) + bx_ref[...])
        a = jnp.exp2(r * rate)
        u = jnp.sqrt(1.0 - a * a) * (i * xc)
        return _scan_segments(a, u, carry_ref[0:1], reverse)

    @pl.when(jnp.logical_or(s == 0, s == n_chunks))
    def _():
        carry_ref[...] = jnp.zeros_like(carry_ref)

    @pl.when(s < n_chunks)
    def _():
        prev = jnp.where(s == 0, 0.0, prev_ref[...])
        nxt = jnp.where(s == n_chunks - 1, 0.0, next_ref[...])
        xc = _conv_segment_major(xbuf_ref, x_ref, prev, nxt, cw_ref[...], cb_ref[...])
        h, c = gates_scan(xc, reverse=False)
        row0 = pl.multiple_of(s * tc, tc)
        xc_ref[pl.ds(row0, tc), :] = xc
        hf_ref[pl.ds(row0, tc), :] = h
        carry_ref[0:1] = c

    @pl.when(s >= n_chunks)
    def _():
        row0 = pl.multiple_of((2 * n_chunks - 1 - s) * tc, tc)
        h, c = gates_scan(xc_ref[pl.ds(row0, tc), :], reverse=True)
        carry_ref[0:1] = c
        tot = hf_ref[pl.ds(row0, tc), :] + h
        for c in range(obuf_ref.shape[0]):
            obuf_ref[c] = tot[:, c * LANES:(c + 1) * LANES]
        o_ref[...] = (_from_segment_major(obuf_ref) * jax.nn.gelu(y_ref[...])).astype(o_ref.dtype)


def lru_branch(z_lru, conv_w, conv_b, wa, ba, wx, bx, lam, batch, seq, tc=512):
    t = z_lru.shape[0]
    w = LRU_BLOCK
    n_chunks = seq // tc
    y_col0 = D_MODEL // w
    last = n_chunks - 1

    def fwd_ix(s):
        return jnp.minimum(s, last)

    def bwd_ix(s):
        return jnp.where(s >= n_chunks, 2 * n_chunks - 1 - s, last)

    def x_map(b, h, s):
        return (b * n_chunks + fwd_ix(s), h)

    def prev_map(b, h, s):
        return (jnp.maximum((b * seq + fwd_ix(s) * tc) // SUBLANES - 1, 0), h)

    def next_map(b, h, s):
        return (jnp.minimum((b * seq + (fwd_ix(s) + 1) * tc) // SUBLANES, t // SUBLANES - 1), h)

    def y_map(b, h, s):
        return (b * n_chunks + bwd_ix(s), y_col0 + h)

    def o_map(b, h, s):
        return (b * n_chunks + bwd_ix(s), h)

    def dir_map(b, h, s):
        return (s // n_chunks, h, 0, 0)

    return pl.pallas_call(
        functools.partial(_lru_kernel, n_chunks=n_chunks, tc=tc),
        grid=(batch, LRU_HEADS, 2 * n_chunks),
        in_specs=[pl.BlockSpec((tc, w), x_map),
                  pl.BlockSpec((SUBLANES, w), prev_map),
                  pl.BlockSpec((SUBLANES, w), next_map),
                  pl.BlockSpec((tc, w), y_map),
                  pl.BlockSpec((CONV_WIDTH, w), lambda b, h, s: (0, h)),
                  pl.BlockSpec((1, w), lambda b, h, s: (0, h)),
                  pl.BlockSpec((None, None, w, w), dir_map),
                  pl.BlockSpec((None, None, 1, w), dir_map),
                  pl.BlockSpec((None, None, w, w), dir_map),
                  pl.BlockSpec((None, None, 1, w), dir_map),
                  pl.BlockSpec((None, None, 1, w), dir_map)],
        out_specs=pl.BlockSpec((tc, w), o_map),
        out_shape=jax.ShapeDtypeStruct((t, D_MODEL), BF16),
        scratch_shapes=[pltpu.VMEM((w // LANES, tc + (CONV_WIDTH - 1) * SUBLANES, LANES), F32),
                        pltpu.VMEM((w // LANES, tc, LANES), F32),
                        pltpu.VMEM((seq, w), F32),
                        pltpu.VMEM((seq, w), F32),
                        pltpu.VMEM((SUBLANES, w), F32)],
        compiler_params=_cparams(("parallel", "parallel", "arbitrary"), 48 << 20),
        name="lru_branch",
    )(z_lru, z_lru, z_lru, z_lru, conv_w, conv_b.reshape(1, -1), wa, ba, wx, bx, lam)


def _q_up_kernel(a_ref, w_ref, c_ref, s_ref, o_ref, *, heads):
    q = _dot(a_ref[...], w_ref[...])
    c = c_ref[...]
    s = s_ref[...]
    for h in range(heads):
        lo = h * QK_PAD
        o_ref[:, lo:lo + LANES] = q[:, lo:lo + LANES].astype(o_ref.dtype)
        o_ref[:, lo + LANES:lo + QK_PAD] = _rope128(q[:, lo + LANES:lo + QK_PAD], c, s).astype(o_ref.dtype)


def q_up(cq, w_q, rope_c, rope_s, seq, tm=1024, heads=8):
    t, k = cq.shape
    n = w_q.shape[1]
    tn = heads * QK_PAD
    nseq = seq // tm
    return pl.pallas_call(
        functools.partial(_q_up_kernel, heads=heads),
        grid=(t // tm, n // tn),
        in_specs=[pl.BlockSpec((tm, k), lambda i, j: (i, 0)),
                  pl.BlockSpec((k, tn), lambda i, j: (0, j)),
                  pl.BlockSpec((tm, LANES), lambda i, j: (i % nseq, 0)),
                  pl.BlockSpec((tm, LANES), lambda i, j: (i % nseq, 0))],
        out_specs=pl.BlockSpec((tm, tn), lambda i, j: (i, j)),
        out_shape=jax.ShapeDtypeStruct((t, n), BF16),
        compiler_params=_cparams(("parallel", "parallel"), 40 << 20),
        name="q_up",
    )(cq, w_q, rope_c, rope_s)


def _kv_up_kernel(a_ref, wk_ref, wv_ref, kpe_ref, k_ref, v_ref, *, heads):
    a = a_ref[...]
    kn = _dot(a, wk_ref[...])
    vv = _dot(a, wv_ref[...])
    kpe = kpe_ref[...]
    ones = jnp.ones((a.shape[0], V_PAD - V_HEAD), v_ref.dtype)
    for h in range(heads):
        k_ref[:, h * QK_PAD:h * QK_PAD + LANES] = kn[:, h * LANES:(h + 1) * LANES].astype(k_ref.dtype)
        k_ref[:, h * QK_PAD + LANES:(h + 1) * QK_PAD] = kpe
        v_ref[:, h * V_PAD:h * V_PAD + V_HEAD] = vv[:, h * V_HEAD:(h + 1) * V_HEAD].astype(v_ref.dtype)
        v_ref[:, h * V_PAD + V_HEAD:(h + 1) * V_PAD] = ones


def kv_up(ckv, w_k, w_v, kpe, tm=1024, heads=8):
    t, k = ckv.shape
    n = w_k.shape[1]
    tn = heads * LANES
    return pl.pallas_call(
        functools.partial(_kv_up_kernel, heads=heads),
        grid=(t // tm, n // tn),
        in_specs=[pl.BlockSpec((tm, k), lambda i, j: (i, 0)),
                  pl.BlockSpec((k, tn), lambda i, j: (0, j)),
                  pl.BlockSpec((k, tn), lambda i, j: (0, j)),
                  pl.BlockSpec((tm, LANES), lambda i, j: (i, 0))],
        out_specs=[pl.BlockSpec((tm, heads * QK_PAD), lambda i, j: (i, j)),
                   pl.BlockSpec((tm, heads * V_PAD), lambda i, j: (i, j))],
        out_shape=[jax.ShapeDtypeStruct((t, MLA_HEADS * QK_PAD), BF16),
                   jax.ShapeDtypeStruct((t, MLA_HEADS * V_PAD), BF16)],
        compiler_params=_cparams(("parallel", "parallel"), 40 << 20),
        name="kv_up",
    )(ckv, w_k, w_v, kpe)


def _attn_kernel(q_ref, k_ref, v_ref, o_ref, *, tk):
    q = q_ref[...]
    nk = k_ref.shape[0] // tk
    c = QK_HEAD ** -0.5 * LOG2E

    def scores(j):
        return lax.dot_general(q, k_ref[j * tk:(j + 1) * tk, :], (((1,), (1,)), ((), ())),
                               preferred_element_type=F32)

    s_next = scores(0)
    m = acc = None
    for j in range(nk):
        s = s_next
        if j + 1 < nk:
            s_next = scores(j + 1)
        m_new = jnp.max(s, axis=-1, keepdims=True)
        if j > 0:
            m_new = jnp.maximum(m, m_new)
        p = jnp.exp2((s - m_new) * c).astype(BF16)
        pv = _dot(p, v_ref[j * tk:(j + 1) * tk, :])
        acc = pv if j == 0 else jnp.exp2((m - m_new) * c) * acc + pv
        m = m_new
    o_ref[...] = (acc[:, :V_HEAD] / acc[:, V_HEAD:]).astype(o_ref.dtype)


def attention(q, k, v, batch, seq, tq=1024, tk=1024):
    t = q.shape[0]
    nq = seq // tq
    return pl.pallas_call(
        functools.partial(_attn_kernel, tk=tk),
        grid=(batch, MLA_HEADS, nq),
        in_specs=[pl.BlockSpec((tq, QK_PAD), lambda b, h, i: (b * nq + i, h)),
                  pl.BlockSpec((seq, QK_PAD), lambda b, h, i: (b, h)),
                  pl.BlockSpec((seq, V_PAD), lambda b, h, i: (b, h))],
        out_specs=pl.BlockSpec((tq, V_HEAD), lambda b, h, i: (b * nq + i, h)),
        out_shape=jax.ShapeDtypeStruct((t, MLA_HEADS * V_HEAD), BF16),
        compiler_params=_cparams(("parallel", "parallel", "arbitrary"), 48 << 20),
        name="attention",
    )(q, k, v)


def _merge_kernel(al_ref, wl_ref, am_ref, wm_ref, ga_ref, gb_ref, o_ref):
    o_lru = _dot(al_ref[...], wl_ref[...])
    o_mla = _dot(am_ref[...], wm_ref[...])
    merged = jax.nn.sigmoid(ga_ref[...]) * o_lru + jax.nn.sigmoid(gb_ref[...]) * o_mla
    o_ref[...] = merged.astype(o_ref.dtype)


def merge_proj(a_lru, w_lru, attn, w_mla, z_gate, tm=1024, tn=256):
    m, k = attn.shape
    n = w_mla.shape[1]
    gb0 = D_MODEL // tn
    return pl.pallas_call(
        _merge_kernel,
        grid=(m // tm, n // tn),
        in_specs=[pl.BlockSpec((tm, k), lambda i, j: (i, 0)),
                  pl.BlockSpec((k, tn), lambda i, j: (0, j)),
                  pl.BlockSpec((tm, k), lambda i, j: (i, 0)),
                  pl.BlockSpec((k, tn), lambda i, j: (0, j)),
                  pl.BlockSpec((tm, tn), lambda i, j: (i, j)),
                  pl.BlockSpec((tm, tn), lambda i, j: (i, gb0 + j))],
        out_specs=pl.BlockSpec((tm, tn), lambda i, j: (i, j)),
        out_shape=jax.ShapeDtypeStruct((m, n), BF16),
        compiler_params=_cparams(("parallel", "parallel"), 56 << 20),
        name="merge_proj",
    )(a_lru, w_lru, attn, w_mla, z_gate, z_gate)


def _mm_res_kernel(a_ref, w_ref, r_ref, o_ref):
    o_ref[...] = r_ref[...] + _dot(a_ref[...], w_ref[...])


def matmul_residual(a, w, res, tm=1024, tn=1024):
    m, k = a.shape
    n = w.shape[1]
    return pl.pallas_call(
        _mm_res_kernel,
        grid=(m // tm, n // tn),
        in_specs=[pl.BlockSpec((tm, k), lambda i, j: (i, 0)),
                  pl.BlockSpec((k, tn), lambda i, j: (0, j)),
                  pl.BlockSpec((tm, tn), lambda i, j: (i, j))],
        out_specs=pl.BlockSpec((tm, tn), lambda i, j: (i, j)),
        out_shape=jax.ShapeDtypeStruct((m, n), F32),
        compiler_params=_cparams(("parallel", "parallel"), 56 << 20),
        name="matmul_residual",
    )(a, w, res)


def _up_kernel(a_ref, w_ref, o_ref):
    o_ref[...] = jnp.square(jnp.maximum(_dot(a_ref[...], w_ref[...]), 0.0)).astype(o_ref.dtype)


def mlp_up(a, w, tm=1024, tn=1024):
    m, k = a.shape
    n = w.shape[1]
    return pl.pallas_call(
        _up_kernel,
        grid=(m // tm, n // tn),
        in_specs=[pl.BlockSpec((tm, k), lambda i, j: (i, 0)),
                  pl.BlockSpec((k, tn), lambda i, j: (0, j))],
        out_specs=pl.BlockSpec((tm, tn), lambda i, j: (i, j)),
        out_shape=jax.ShapeDtypeStruct((m, n), BF16),
        compiler_params=_cparams(("parallel", "parallel"), 56 << 20),
        name="mlp_up",
    )(a, w)


def _down_kernel(a_ref, w_ref, r_ref, o_ref, acc_ref):
    kk = pl.program_id(2)

    @pl.when(kk == 0)
    def _():
        acc_ref[...] = r_ref[...]

    acc_ref[...] += _dot(a_ref[...], w_ref[...])

    @pl.when(kk == pl.num_programs(2) - 1)
    def _():
        o_ref[...] = acc_ref[...]


def mlp_down(a, w, res, tm=1024, tn=1024, tk=2048):
    m, k = a.shape
    n = w.shape[1]
    return pl.pallas_call(
        _down_kernel,
        grid=(m // tm, n // tn, k // tk),
        in_specs=[pl.BlockSpec((tm, tk), lambda i, j, kk: (i, kk)),
                  pl.BlockSpec((tk, tn), lambda i, j, kk: (kk, j)),
                  pl.BlockSpec((tm, tn), lambda i, j, kk: (i, j))],
        out_specs=pl.BlockSpec((tm, tn), lambda i, j, kk: (i, j)),
        out_shape=jax.ShapeDtypeStruct((m, n), F32),
        scratch_shapes=[pltpu.VMEM((tm, tn), F32)],
        compiler_params=_cparams(("parallel", "parallel", "arbitrary"), 56 << 20),
        name="mlp_down",
    )(a, w, res)


def _rope_tables(seq):
    inv = 1.0 / (ROPE_THETA ** (jnp.arange(0, QK_ROPE, 2, dtype=F32) / QK_ROPE))
    ang = jnp.arange(seq, dtype=F32)[:, None] * inv[None, :]
    cos, sin = jnp.cos(ang), jnp.sin(ang)
    zero = jnp.zeros((seq, LANES - QK_ROPE), F32)
    return (jnp.concatenate([cos, cos, zero], axis=1),
            jnp.concatenate([-sin, sin, zero], axis=1))


def _prep_weights(w_in, lru_wa, lru_ba, lru_wx, lru_bx, lru_lam, w_q_up, w_kv_up,
                  w_lru_proj, w_mla_proj, w_out, w_up, w_down):
    d = D_MODEL
    small_lo, small_hi = 2 * d, 2 * d + Q_LORA + KV_LORA + QK_ROPE
    small = w_in[:, small_lo:small_hi]
    w_small = jnp.pad(small, ((0, 0), (0, SMALL_COLS - small.shape[1]))).astype(BF16)
    wq = w_q_up.reshape(Q_LORA, MLA_HEADS, QK_HEAD)
    wq = jnp.pad(wq, ((0, 0), (0, 0), (0, QK_PAD - QK_HEAD))).reshape(Q_LORA, MLA_HEADS * QK_PAD)
    wkv = w_kv_up.reshape(KV_LORA, MLA_HEADS, QK_NOPE + V_HEAD)
    wk = wkv[:, :, :QK_NOPE].reshape(KV_LORA, MLA_HEADS * QK_NOPE)
    wv = wkv[:, :, QK_NOPE:].reshape(KV_LORA, MLA_HEADS * V_HEAD)
    shp = (2, LRU_HEADS, 1, LRU_BLOCK)
    return dict(
        w_lru_in=w_in[:, :small_lo].astype(BF16), w_gate_in=w_in[:, small_hi:].astype(BF16),
        w_small=w_small, wq=wq.astype(BF16), wk=wk.astype(BF16), wv=wv.astype(BF16),
        wa=lru_wa.astype(BF16), wx=lru_wx.astype(BF16),
        ba=lru_ba.reshape(shp), bx=lru_bx.reshape(shp), lam=lru_lam.reshape(shp),
        w_lru_proj=w_lru_proj.astype(BF16), w_mla_proj=w_mla_proj.astype(BF16),
        w_out=w_out.astype(BF16), w_up=w_up.astype(BF16), w_down=w_down.astype(BF16))


def _layer(x3, p, norm1, conv_w, conv_b, q_norm, kv_norm, norm2, norm_f):
    batch, seq, d = x3.shape
    x = x3.reshape(batch * seq, d)
    rope_c, rope_s = _rope_tables(seq)

    xn = rmsnorm(x, norm1, BF16)
    z_lru = matmul(xn, p["w_lru_in"], F32, name="in_proj_lru")
    z_gate = matmul(xn, p["w_gate_in"], F32, name="in_proj_gate")
    cq, ckv, kpe = small_proj(xn, p["w_small"], q_norm, kv_norm, rope_c, rope_s, seq)

    a_lru = lru_branch(z_lru, conv_w, conv_b, p["wa"], p["ba"], p["wx"], p["bx"], p["lam"], batch, seq)

    q = q_up(cq, p["wq"], rope_c, rope_s, seq)
    k, v = kv_up(ckv, p["wk"], p["wv"], kpe)
    attn = attention(q, k, v, batch, seq)

    merged = merge_proj(a_lru, p["w_lru_proj"], attn, p["w_mla_proj"], z_gate)
    h = matmul_residual(merged, p["w_out"], x)

    hn = rmsnorm(h, norm2, BF16)
    u = mlp_up(hn, p["w_up"])
    h2 = mlp_down(u, p["w_down"], h)
    return rmsnorm(h2, norm_f, F32).reshape(batch, seq, d)


def kernel(x_prompt, x_sample, norm1, w_in, conv_w, conv_b, lru_wa, lru_ba, lru_wx, lru_bx, lru_lam,
           q_norm, w_q_up, kv_norm, w_kv_up, w_lru_proj, w_mla_proj, w_out, norm2, w_up, w_down, norm_f):
    assert norm1.shape[0] == 1, "single-layer trunk"
    p = _prep_weights(w_in[0], lru_wa[0], lru_ba[0], lru_wx[0], lru_bx[0], lru_lam[0], w_q_up[0],
                      w_kv_up[0], w_lru_proj[0], w_mla_proj[0], w_out[0], w_up[0], w_down[0])
    args = (p, norm1[0], conv_w[0], conv_b[0], q_norm[0], kv_norm[0], norm2[0], norm_f)
    return (_layer(x_prompt, *args), _layer(x_sample, *args))
```

```python
import functools
import math

import jax
import jax.numpy as jnp
from jax import lax
from jax.experimental import pallas as pl
from jax.experimental.pallas import tpu as pltpu

D_MODEL = 4096
LRU_HEADS = 16
LRU_BLOCK = 256
CONV_WIDTH = 4
CONV_LEFT = 2
LRU_C = 8.0
MLA_HEADS = 32
Q_LORA = 1024
KV_LORA = 512
QK_NOPE = 128
QK_ROPE = 64
V_HEAD = 128
QK_HEAD = QK_NOPE + QK_ROPE
ROPE_THETA = 10000.0
D_FF = 4 * D_MODEL
EPS = 1e-6
LOG2E = math.log2(math.e)

LANES = 128
SUBLANES = 8
QK_PAD = 2 * LANES
V_PAD = 2 * LANES
SMALL_COLS = 13 * LANES
VMEM_CAP = 60 * 1024 * 1024

BF16 = jnp.bfloat16
F32 = jnp.float32


def _cparams(semantics, vmem_bytes):
    return pltpu.CompilerParams(dimension_semantics=semantics,
                                vmem_limit_bytes=min(int(vmem_bytes), VMEM_CAP))


def _dot(a, b):
    return jnp.dot(a, b, preferred_element_type=F32)


def _rms(x, g):
    y = x * lax.rsqrt(jnp.mean(jnp.square(x), axis=-1, keepdims=True) + EPS)
    return y * g


def _rmsnorm_kernel(x_ref, g_ref, o_ref):
    o_ref[...] = _rms(x_ref[...], g_ref[...]).astype(o_ref.dtype)


def rmsnorm(x, g, out_dtype, tr=256):
    t, d = x.shape
    return pl.pallas_call(
        _rmsnorm_kernel,
        grid=(t // tr,),
        in_specs=[pl.BlockSpec((tr, d), lambda i: (i, 0)),
                  pl.BlockSpec((1, d), lambda i: (0, 0))],
        out_specs=pl.BlockSpec((tr, d), lambda i: (i, 0)),
        out_shape=jax.ShapeDtypeStruct((t, d), out_dtype),
        compiler_params=_cparams(("parallel",), 40 << 20),
        name="rmsnorm",
    )(x, g.reshape(1, d))


def _mm_kernel(a_ref, b_ref, o_ref):
    o_ref[...] = _dot(a_ref[...], b_ref[...]).astype(o_ref.dtype)


def matmul(a, b, out_dtype, tm=1024, tn=1024, name="matmul"):
    m, k = a.shape
    n = b.shape[1]
    return pl.pallas_call(
        _mm_kernel,
        grid=(m // tm, n // tn),
        in_specs=[pl.BlockSpec((tm, k), lambda i, j: (i, 0)),
                  pl.BlockSpec((k, tn), lambda i, j: (0, j))],
        out_specs=pl.BlockSpec((tm, tn), lambda i, j: (i, j)),
        out_shape=jax.ShapeDtypeStruct((m, n), out_dtype),
        compiler_params=_cparams(("parallel", "parallel"), 56 << 20),
        name=name,
    )(a, b)


def _rope128(x, c, s):
    lane = lax.broadcasted_iota(jnp.int32, x.shape, 1)
    swapped = jnp.where(lane < QK_ROPE // 2,
                        pltpu.roll(x, LANES - QK_ROPE // 2, 1),
                        pltpu.roll(x, QK_ROPE // 2, 1))
    return x * c + swapped * s


def _small_proj_kernel(a_ref, w_ref, qn_ref, kvn_ref, c_ref, s_ref, cq_ref, ckv_ref, kpe_ref):
    z = _dot(a_ref[...], w_ref[...])
    cq_ref[...] = _rms(z[:, :Q_LORA], qn_ref[...]).astype(cq_ref.dtype)
    ckv_ref[...] = _rms(z[:, Q_LORA:Q_LORA + KV_LORA], kvn_ref[...]).astype(ckv_ref.dtype)
    kr = z[:, Q_LORA + KV_LORA:]
    kpe_ref[...] = _rope128(kr, c_ref[...], s_ref[...]).astype(kpe_ref.dtype)


def small_proj(xn, w_small, q_norm, kv_norm, rope_c, rope_s, seq, tm=512):
    t, d = xn.shape
    nseq = seq // tm
    return pl.pallas_call(
        _small_proj_kernel,
        grid=(t // tm,),
        in_specs=[pl.BlockSpec((tm, d), lambda i: (i, 0)),
                  pl.BlockSpec((d, SMALL_COLS), lambda i: (0, 0)),
                  pl.BlockSpec((1, Q_LORA), lambda i: (0, 0)),
                  pl.BlockSpec((1, KV_LORA), lambda i: (0, 0)),
                  pl.BlockSpec((tm, LANES), lambda i: (i % nseq, 0)),
                  pl.BlockSpec((tm, LANES), lambda i: (i % nseq, 0))],
        out_specs=[pl.BlockSpec((tm, Q_LORA), lambda i: (i, 0)),
                   pl.BlockSpec((tm, KV_LORA), lambda i: (i, 0)),
                   pl.BlockSpec((tm, LANES), lambda i: (i, 0))],
        out_shape=[jax.ShapeDtypeStruct((t, Q_LORA), BF16),
                   jax.ShapeDtypeStruct((t, KV_LORA), BF16),
                   jax.ShapeDtypeStruct((t, LANES), BF16)],
        compiler_params=_cparams(("parallel",), 56 << 20),
        name="small_proj",
    )(xn, w_small, q_norm.reshape(1, -1), kv_norm.reshape(1, -1), rope_c, rope_s)


def _softplus(x):
    return jnp.maximum(x, 0.0) + jnp.log1p(jnp.exp(-jnp.abs(x)))


def _segment_rows(tc):
    n = tc // SUBLANES
    return [j0 * SUBLANES + i for i, j0 in (divmod(m * SUBLANES, n) for m in range(n))]


def _to_segment_major(buf_ref, slot0, src_ref):
    tc, w = src_ref.shape
    for c in range(w // LANES):
        for m, row in enumerate(_segment_rows(tc)):
            buf_ref.at[c][pl.ds(slot0 * SUBLANES + row, SUBLANES, stride=SUBLANES), :] = (
                src_ref[m * SUBLANES:(m + 1) * SUBLANES, c * LANES:(c + 1) * LANES])


def _from_segment_major(buf_ref):
    tc = buf_ref.shape[1]
    return jnp.concatenate(
        [jnp.concatenate([buf_ref.at[c][pl.ds(row, SUBLANES, stride=SUBLANES), :]
                          for row in _segment_rows(tc)], axis=0)
         for c in range(buf_ref.shape[0])], axis=1)


def _conv_segment_major(xbuf_ref, x_ref, prev, nxt, cw, cb):
    tc = x_ref.shape[0]
    n = tc // SUBLANES
    _to_segment_major(xbuf_ref, CONV_LEFT, x_ref)
    sub = lax.broadcasted_iota(jnp.int32, (SUBLANES, LANES), 0)
    out = []
    for c in range(xbuf_ref.shape[0]):
        lanes = slice(c * LANES, (c + 1) * LANES)

        def slot(j):
            return xbuf_ref[c, (j + CONV_LEFT) * SUBLANES:(j + CONV_LEFT + 1) * SUBLANES, :]

        for back in range(1, CONV_LEFT + 1):
            halo = jnp.broadcast_to(prev[SUBLANES - back:SUBLANES - back + 1, lanes], sub.shape)
            xbuf_ref[c, (CONV_LEFT - back) * SUBLANES:(CONV_LEFT - back + 1) * SUBLANES, :] = jnp.where(
                sub == 0, halo, pltpu.roll(slot(n - back), 1, 0))
        for fwd in range(CONV_WIDTH - 1 - CONV_LEFT):
            halo = jnp.broadcast_to(nxt[fwd:fwd + 1, lanes], sub.shape)
            xbuf_ref[c, (n + CONV_LEFT + fwd) * SUBLANES:(n + CONV_LEFT + fwd + 1) * SUBLANES, :] = jnp.where(
                sub == SUBLANES - 1, halo, pltpu.roll(slot(fwd), SUBLANES - 1, 0))
        xc = cb[:, lanes]
        for k in range(CONV_WIDTH):
            xc = xc + xbuf_ref[c, k * SUBLANES:k * SUBLANES + tc, :] * cw[k:k + 1, lanes]
        out.append(xc)
    return jnp.concatenate(out, axis=1)


def _scan_segments(a, u, c_in, reverse):
    n = a.shape[0] // SUBLANES
    local = [None] * n
    prod = [None] * n
    h = p = None
    for j in (range(n - 1, -1, -1) if reverse else range(n)):
        aj = a[j * SUBLANES:(j + 1) * SUBLANES]
        uj = u[j * SUBLANES:(j + 1) * SUBLANES]
        if h is None:
            h, p = uj, aj
        else:
            h, p = aj * h + uj, aj * p
        local[j], prod[j] = h, p
    carries = [None] * SUBLANES
    c = c_in
    for i in (range(SUBLANES - 1, -1, -1) if reverse else range(SUBLANES)):
        carries[i] = c
        c = h[i:i + 1] + p[i:i + 1] * c
    cmat = jnp.concatenate(carries, axis=0)
    out = jnp.concatenate([local[j] + prod[j] * cmat for j in range(n)], axis=0)
    return out, c


def _lru_kernel(x_ref, prev_ref, next_ref, y_ref, cw_ref, cb_ref, wa_ref, ba_ref, wx_ref, bx_ref,
                lam_ref, o_ref, xbuf_ref, obuf_ref, xc_ref, hf_ref, carry_ref, *, n_chunks, tc):
    s = pl.program_id(2)
    rate = (-LRU_C * LOG2E) * _softplus(-lam_ref[...])

    def gates_scan(xc, reverse):
        xcb = xc.astype(BF16)
        r = jax.nn.sigmoid(_dot(xcb, wa_ref[...]) + ba_ref[...])
        i = jax.nn.sigmoid(_dot(xcb, wx_ref[...]) + bx_ref[...])
        a = jnp.exp2(r * rate)
        u = jnp.sqrt(1.0 - a * a) * (i * xc)
        return _scan_segments(a, u, carry_ref[0:1], reverse)

    @pl.when(jnp.logical_or(s == 0, s == n_chunks))
    def _():
        carry_ref[...] = jnp.zeros_like(carry_ref)

    @pl.when(s < n_chunks)
    def _():
        prev = jnp.where(s == 0, 0.0, prev_ref[...])
        nxt = jnp.where(s == n_chunks - 1, 0.0, next_ref[...])
        xc = _conv_segment_major(xbuf_ref, x_ref, prev, nxt, cw_ref[...], cb_ref[...])
        h, c = gates_scan(xc, reverse=False)
        row0 = pl.multiple_of(s * tc, tc)
        xc_ref[pl.ds(row0, tc), :] = xc
        hf_ref[pl.ds(row0, tc), :] = h
        carry_ref[0:1] = c

    @pl.when(s >= n_chunks)
    def _():
        row0 = pl.multiple_of((2 * n_chunks - 1 - s) * tc, tc)
        h, c = gates_scan(xc_ref[pl.ds(row0, tc), :], reverse=True)
        carry_ref[0:1] = c
        tot = hf_ref[pl.ds(row0, tc), :] + h
        for half in range(obuf_ref.shape[0]):
            obuf_ref[half] = tot[:, half * LANES:(half + 1) * LANES]
        o_ref[...] = (_from_segment_major(obuf_ref) * jax.nn.gelu(y_ref[...])).astype(o_ref.dtype)


def lru_branch(z_lru, conv_w, conv_b, wa, ba, wx, bx, lam, batch, seq, tc=1024):
    t = z_lru.shape[0]
    w = LRU_BLOCK
    n_chunks = seq // tc
    y_col0 = D_MODEL // w
    last = n_chunks - 1

    def fwd_ix(s):
        return jnp.minimum(s, last)

    def bwd_ix(s):
        return jnp.where(s >= n_chunks, 2 * n_chunks - 1 - s, last)

    def x_map(b, h, s):
        return (b * n_chunks + fwd_ix(s), h)

    def prev_map(b, h, s):
        return (jnp.maximum((b * seq + fwd_ix(s) * tc) // SUBLANES - 1, 0), h)

    def next_map(b, h, s):
        return (jnp.minimum((b * seq + (fwd_ix(s) + 1) * tc) // SUBLANES, t // SUBLANES - 1), h)

    def y_map(b, h, s):
        return (b * n_chunks + bwd_ix(s), y_col0 + h)

    def o_map(b, h, s):
        return (b * n_chunks + bwd_ix(s), h)

    def dir_map(b, h, s):
        return (s // n_chunks, h, 0, 0)

    return pl.pallas_call(
        functools.partial(_lru_kernel, n_chunks=n_chunks, tc=tc),
        grid=(batch, LRU_HEADS, 2 * n_chunks),
        in_specs=[pl.BlockSpec((tc, w), x_map),
                  pl.BlockSpec((SUBLANES, w), prev_map),
                  pl.BlockSpec((SUBLANES, w), next_map),
                  pl.BlockSpec((tc, w), y_map),
                  pl.BlockSpec((CONV_WIDTH, w), lambda b, h, s: (0, h)),
                  pl.BlockSpec((1, w), lambda b, h, s: (0, h)),
                  pl.BlockSpec((None, None, w, w), dir_map),
                  pl.BlockSpec((None, None, 1, w), dir_map),
                  pl.BlockSpec((None, None, w, w), dir_map),
                  pl.BlockSpec((None, None, 1, w), dir_map),
                  pl.BlockSpec((None, None, 1, w), dir_map)],
        out_specs=pl.BlockSpec((tc, w), o_map),
        out_shape=jax.ShapeDtypeStruct((t, D_MODEL), BF16),
        scratch_shapes=[pltpu.VMEM((w // LANES, tc + (CONV_WIDTH - 1) * SUBLANES, LANES), F32),
                        pltpu.VMEM((w // LANES, tc, LANES), F32),
                        pltpu.VMEM((seq, w), F32),
                        pltpu.VMEM((seq, w), F32),
                        pltpu.VMEM((SUBLANES, w), F32)],
        compiler_params=_cparams(("parallel", "parallel", "arbitrary"), 48 << 20),
        name="lru_branch",
    )(z_lru, z_lru, z_lru, z_lru, conv_w, conv_b.reshape(1, -1), wa, ba, wx, bx, lam)


def _q_up_kernel(a_ref, w_ref, c_ref, s_ref, o_ref, *, heads):
    q = _dot(a_ref[...], w_ref[...])
    c = c_ref[...]
    s = s_ref[...]
    for h in range(heads):
        lo = h * QK_PAD
        o_ref[:, lo:lo + LANES] = q[:, lo:lo + LANES].astype(o_ref.dtype)
        o_ref[:, lo + LANES:lo + QK_PAD] = _rope128(q[:, lo + LANES:lo + QK_PAD], c, s).astype(o_ref.dtype)


def q_up(cq, w_q, rope_c, rope_s, seq, tm=1024, heads=8):
    t, k = cq.shape
    n = w_q.shape[1]
    tn = heads * QK_PAD
    nseq = seq // tm
    return pl.pallas_call(
        functools.partial(_q_up_kernel, heads=heads),
        grid=(t // tm, n // tn),
        in_specs=[pl.BlockSpec((tm, k), lambda i, j: (i, 0)),
                  pl.BlockSpec((k, tn), lambda i, j: (0, j)),
                  pl.BlockSpec((tm, LANES), lambda i, j: (i % nseq, 0)),
                  pl.BlockSpec((tm, LANES), lambda i, j: (i % nseq, 0))],
        out_specs=pl.BlockSpec((tm, tn), lambda i, j: (i, j)),
        out_shape=jax.ShapeDtypeStruct((t, n), BF16),
        compiler_params=_cparams(("parallel", "parallel"), 40 << 20),
        name="q_up",
    )(cq, w_q, rope_c, rope_s)


def _kv_up_kernel(a_ref, wk_ref, wv_ref, kpe_ref, k_ref, v_ref, *, heads):
    a = a_ref[...]
    kn = _dot(a, wk_ref[...])
    vv = _dot(a, wv_ref[...])
    kpe = kpe_ref[...]
    ones = jnp.ones((a.shape[0], V_PAD - V_HEAD), v_ref.dtype)
    for h in range(heads):
        k_ref[:, h * QK_PAD:h * QK_PAD + LANES] = kn[:, h * LANES:(h + 1) * LANES].astype(k_ref.dtype)
        k_ref[:, h * QK_PAD + LANES:(h + 1) * QK_PAD] = kpe
        v_ref[:, h * V_PAD:h * V_PAD + V_HEAD] = vv[:, h * V_HEAD:(h + 1) * V_HEAD].astype(v_ref.dtype)
        v_ref[:, h * V_PAD + V_HEAD:(h + 1) * V_PAD] = ones


def kv_up(ckv, w_k, w_v, kpe, tm=1024, heads=8):
    t, k = ckv.shape
    n = w_k.shape[1]
    tn = heads * LANES
    return pl.pallas_call(
        functools.partial(_kv_up_kernel, heads=heads),
        grid=(t // tm, n // tn),
        in_specs=[pl.BlockSpec((tm, k), lambda i, j: (i, 0)),
                  pl.BlockSpec((k, tn), lambda i, j: (0, j)),
                  pl.BlockSpec((k, tn), lambda i, j: (0, j)),
                  pl.BlockSpec((tm, LANES), lambda i, j: (i, 0))],
        out_specs=[pl.BlockSpec((tm, heads * QK_PAD), lambda i, j: (i, j)),
                   pl.BlockSpec((tm, heads * V_PAD), lambda i, j: (i, j))],
        out_shape=[jax.ShapeDtypeStruct((t, MLA_HEADS * QK_PAD), BF16),
                   jax.ShapeDtypeStruct((t, MLA_HEADS * V_PAD), BF16)],
        compiler_params=_cparams(("parallel", "parallel"), 40 << 20),
        name="kv_up",
    )(ckv, w_k, w_v, kpe)


def _attn_kernel(q_ref, k_ref, v_ref, o_ref, *, tk):
    q = q_ref[...]
    nk = k_ref.shape[0] // tk
    c = QK_HEAD ** -0.5 * LOG2E

    def scores(j):
        return lax.dot_general(q, k_ref[j * tk:(j + 1) * tk, :], (((1,), (1,)), ((), ())),
                               preferred_element_type=F32)

    s_next = scores(0)
    m = acc = None
    for j in range(nk):
        s = s_next
        if j + 1 < nk:
            s_next = scores(j + 1)
        m_new = jnp.max(s, axis=-1, keepdims=True)
        if j > 0:
            m_new = jnp.maximum(m, m_new)
        p = jnp.exp2((s - m_new) * c).astype(BF16)
        pv = _dot(p, v_ref[j * tk:(j + 1) * tk, :])
        acc = pv if j == 0 else jnp.exp2((m - m_new) * c) * acc + pv
        m = m_new
    o_ref[...] = (acc[:, :V_HEAD] / acc[:, V_HEAD:]).astype(o_ref.dtype)


def attention(q, k, v, batch, seq, tq=1024, tk=1024):
    t = q.shape[0]
    nq = seq // tq
    return pl.pallas_call(
        functools.partial(_attn_kernel, tk=tk),
        grid=(batch, MLA_HEADS, nq),
        in_specs=[pl.BlockSpec((tq, QK_PAD), lambda b, h, i: (b * nq + i, h)),
                  pl.BlockSpec((seq, QK_PAD), lambda b, h, i: (b, h)),
                  pl.BlockSpec((seq, V_PAD), lambda b, h, i: (b, h))],
        out_specs=pl.BlockSpec((tq, V_HEAD), lambda b, h, i: (b * nq + i, h)),
        out_shape=jax.ShapeDtypeStruct((t, MLA_HEADS * V_HEAD), BF16),
        compiler_params=_cparams(("parallel", "parallel", "arbitrary"), 48 << 20),
        name="attention",
    )(q, k, v)


def _merge_kernel(al_ref, wl_ref, am_ref, wm_ref, ga_ref, gb_ref, o_ref):
    o_lru = _dot(al_ref[...], wl_ref[...])
    o_mla = _dot(am_ref[...], wm_ref[...])
    merged = jax.nn.sigmoid(ga_ref[...]) * o_lru + jax.nn.sigmoid(gb_ref[...]) * o_mla
    o_ref[...] = merged.astype(o_ref.dtype)


def merge_proj(a_lru, w_lru, attn, w_mla, z_gate, tm=1024, tn=256):
    m, k = attn.shape
    n = w_mla.shape[1]
    gb0 = D_MODEL // tn
    return pl.pallas_call(
        _merge_kernel,
        grid=(m // tm, n // tn),
        in_specs=[pl.BlockSpec((tm, k), lambda i, j: (i, 0)),
                  pl.BlockSpec((k, tn), lambda i, j: (0, j)),
                  pl.BlockSpec((tm, k), lambda i, j: (i, 0)),
                  pl.BlockSpec((k, tn), lambda i, j: (0, j)),
                  pl.BlockSpec((tm, tn), lambda i, j: (i, j)),
                  pl.BlockSpec((tm, tn), lambda i, j: (i, gb0 + j))],
        out_specs=pl.BlockSpec((tm, tn), lambda i, j: (i, j)),
        out_shape=jax.ShapeDtypeStruct((m, n), BF16),
        compiler_params=_cparams(("parallel", "parallel"), 56 << 20),
        name="merge_proj",
    )(a_lru, w_lru, attn, w_mla, z_gate, z_gate)


def _mm_res_kernel(a_ref, w_ref, r_ref, o_ref):
    o_ref[...] = r_ref[...] + _dot(a_ref[...], w_ref[...])


def matmul_residual(a, w, res, tm=1024, tn=1024):
    m, k = a.shape
    n = w.shape[1]
    return pl.pallas_call(
        _mm_res_kernel,
        grid=(m // tm, n // tn),
        in_specs=[pl.BlockSpec((tm, k), lambda i, j: (i, 0)),
                  pl.BlockSpec((k, tn), lambda i, j: (0, j)),
                  pl.BlockSpec((tm, tn), lambda i, j: (i, j))],
        out_specs=pl.BlockSpec((tm, tn), lambda i, j: (i, j)),
        out_shape=jax.ShapeDtypeStruct((m, n), F32),
        compiler_params=_cparams(("parallel", "parallel"), 56 << 20),
        name="matmul_residual",
    )(a, w, res)


def _up_kernel(a_ref, w_ref, o_ref):
    o_ref[...] = jnp.square(jnp.maximum(_dot(a_ref[...], w_ref[...]), 0.0)).astype(o_ref.dtype)


def mlp_up(a, w, tm=1024, tn=1024):
    m, k = a.shape
    n = w.shape[1]
    return pl.pallas_call(
        _up_kernel,
        grid=(m // tm, n // tn),
        in_specs=[pl.BlockSpec((tm, k), lambda i, j: (i, 0)),
                  pl.BlockSpec((k, tn), lambda i, j: (0, j))],
        out_specs=pl.BlockSpec((tm, tn), lambda i, j: (i, j)),
        out_shape=jax.ShapeDtypeStruct((m, n), BF16),
        compiler_params=_cparams(("parallel", "parallel"), 56 << 20),
        name="mlp_up",
    )(a, w)


def _down_kernel(a_ref, w_ref, r_ref, o_ref):
    kk = pl.program_id(2)

    @pl.when(kk == 0)
    def _():
        o_ref[...] = r_ref[...] + _dot(a_ref[...], w_ref[...])

    @pl.when(kk > 0)
    def _():
        o_ref[...] += _dot(a_ref[...], w_ref[...])


def mlp_down(a, w, res, tm=1024, tn=1024, tk=4096):
    m, k = a.shape
    n = w.shape[1]
    return pl.pallas_call(
        _down_kernel,
        grid=(m // tm, n // tn, k // tk),
        in_specs=[pl.BlockSpec((tm, tk), lambda i, j, kk: (i, kk)),
                  pl.BlockSpec((tk, tn), lambda i, j, kk: (kk, j)),
                  pl.BlockSpec((tm, tn), lambda i, j, kk: (i, j))],
        out_specs=pl.BlockSpec((tm, tn), lambda i, j, kk: (i, j)),
        out_shape=jax.ShapeDtypeStruct((m, n), F32),
        compiler_params=_cparams(("parallel", "parallel", "arbitrary"), 56 << 20),
        name="mlp_down",
    )(a, w, res)


def _rope_tables(seq):
    inv = 1.0 / (ROPE_THETA ** (jnp.arange(0, QK_ROPE, 2, dtype=F32) / QK_ROPE))
    ang = jnp.arange(seq, dtype=F32)[:, None] * inv[None, :]
    cos, sin = jnp.cos(ang), jnp.sin(ang)
    zero = jnp.zeros((seq, LANES - QK_ROPE), F32)
    return (jnp.concatenate([cos, cos, zero], axis=1),
            jnp.concatenate([-sin, sin, zero], axis=1))


def _prep_weights(w_in, lru_wa, lru_ba, lru_wx, lru_bx, lru_lam, w_q_up, w_kv_up,
                  w_lru_proj, w_mla_proj, w_out, w_up, w_down):
    d = D_MODEL
    small_lo, small_hi = 2 * d, 2 * d + Q_LORA + KV_LORA + QK_ROPE
    small = w_in[:, small_lo:small_hi]
    w_small = jnp.pad(small, ((0, 0), (0, SMALL_COLS - small.shape[1]))).astype(BF16)
    wq = w_q_up.reshape(Q_LORA, MLA_HEADS, QK_HEAD)
    wq = jnp.pad(wq, ((0, 0), (0, 0), (0, QK_PAD - QK_HEAD))).reshape(Q_LORA, MLA_HEADS * QK_PAD)
    wkv = w_kv_up.reshape(KV_LORA, MLA_HEADS, QK_NOPE + V_HEAD)
    wk = wkv[:, :, :QK_NOPE].reshape(KV_LORA, MLA_HEADS * QK_NOPE)
    wv = wkv[:, :, QK_NOPE:].reshape(KV_LORA, MLA_HEADS * V_HEAD)
    shp = (2, LRU_HEADS, 1, LRU_BLOCK)
    return dict(
        w_lru_in=w_in[:, :small_lo].astype(BF16), w_gate_in=w_in[:, small_hi:].astype(BF16),
        w_small=w_small, wq=wq.astype(BF16), wk=wk.astype(BF16), wv=wv.astype(BF16),
        wa=lru_wa.astype(BF16), wx=lru_wx.astype(BF16),
        ba=lru_ba.reshape(shp), bx=lru_bx.reshape(shp), lam=lru_lam.reshape(shp),
        w_lru_proj=w_lru_proj.astype(BF16), w_mla_proj=w_mla_proj.astype(BF16),
        w_out=w_out.astype(BF16), w_up=w_up.astype(BF16), w_down=w_down.astype(BF16))


def _layer(x3, p, norm1, conv_w, conv_b, q_norm, kv_norm, norm2, norm_f):
    batch, seq, d = x3.shape
    x = x3.reshape(batch * seq, d)
    rope_c, rope_s = _rope_tables(seq)

    xn = rmsnorm(x, norm1, BF16)
    z_lru = matmul(xn, p["w_lru_in"], F32, name="in_proj_lru")
    z_gate = matmul(xn, p["w_gate_in"], F32, name="in_proj_gate")
    cq, ckv, kpe = small_proj(xn, p["w_small"], q_norm, kv_norm, rope_c, rope_s, seq)

    a_lru = lru_branch(z_lru, conv_w, conv_b, p["wa"], p["ba"], p["wx"], p["bx"], p["lam"], batch, seq)

    q = q_up(cq, p["wq"], rope_c, rope_s, seq)
    k, v = kv_up(ckv, p["wk"], p["wv"], kpe)
    attn = attention(q, k, v, batch, seq, tk=min(2048, seq // 2))

    merged = merge_proj(a_lru, p["w_lru_proj"], attn, p["w_mla_proj"], z_gate)
    h = matmul_residual(merged, p["w_out"], x)

    hn = rmsnorm(h, norm2, BF16)
    u = mlp_up(hn, p["w_up"])
    h2 = mlp_down(u, p["w_down"], h)
    return rmsnorm(h2, norm_f, F32).reshape(batch, seq, d)


def kernel(x_prompt, x_sample, norm1, w_in, conv_w, conv_b, lru_wa, lru_ba, lru_wx, lru_bx, lru_lam,
           q_norm, w_q_up, kv_norm, w_kv_up, w_lru_proj, w_mla_proj, w_out, norm2, w_up, w_down, norm_f):
    assert norm1.shape[0] == 1, "single-layer trunk"
    p = _prep_weights(w_in[0], lru_wa[0], lru_ba[0], lru_wx[0], lru_bx[0], lru_lam[0], w_q_up[0],
                      w_kv_up[0], w_lru_proj[0], w_mla_proj[0], w_out[0], w_up[0], w_down[0])
    args = (p, norm1[0], conv_w[0], conv_b[0], q_norm[0], kv_norm[0], norm2[0], norm_f)
    return (_layer(x_prompt, *args), _layer(x_sample, *args))
```

```python
import functools
import math

import jax
import jax.numpy as jnp
from jax import lax
from jax.experimental import pallas as pl
from jax.experimental.pallas import tpu as pltpu

D_MODEL = 4096
LRU_HEADS = 16
LRU_BLOCK = 256
CONV_WIDTH = 4
CONV_LEFT = 2
LRU_C = 8.0
MLA_HEADS = 32
Q_LORA = 1024
KV_LORA = 512
QK_NOPE = 128
QK_ROPE = 64
V_HEAD = 128
QK_HEAD = QK_NOPE + QK_ROPE
ROPE_THETA = 10000.0
D_FF = 4 * D_MODEL
EPS = 1e-6
LOG2E = math.log2(math.e)

LANES = 128
SUBLANES = 8
QK_PAD = 2 * LANES
V_PAD = 2 * LANES
SMALL_COLS = 13 * LANES
VMEM_CAP = 60 * 1024 * 1024

BF16 = jnp.bfloat16
F32 = jnp.float32


def _cparams(semantics, vmem_bytes):
    return pltpu.CompilerParams(dimension_semantics=semantics,
                                vmem_limit_bytes=min(int(vmem_bytes), VMEM_CAP))


def _dot(a, b):
    return jnp.dot(a, b, preferred_element_type=F32)


def _rms(x, g):
    y = x * lax.rsqrt(jnp.mean(jnp.square(x), axis=-1, keepdims=True) + EPS)
    return y * g


def _rmsnorm_kernel(x_ref, g_ref, o_ref):
    o_ref[...] = _rms(x_ref[...], g_ref[...]).astype(o_ref.dtype)


def rmsnorm(x, g, out_dtype, tr=256):
    t, d = x.shape
    return pl.pallas_call(
        _rmsnorm_kernel,
        grid=(t // tr,),
        in_specs=[pl.BlockSpec((tr, d), lambda i: (i, 0)),
                  pl.BlockSpec((1, d), lambda i: (0, 0))],
        out_specs=pl.BlockSpec((tr, d), lambda i: (i, 0)),
        out_shape=jax.ShapeDtypeStruct((t, d), out_dtype),
        compiler_params=_cparams(("parallel",), 40 << 20),
        name="rmsnorm",
    )(x, g.reshape(1, d))


def _mm_kernel(a_ref, b_ref, o_ref):
    o_ref[...] = _dot(a_ref[...], b_ref[...]).astype(o_ref.dtype)


def matmul(a, b, out_dtype, tm=1024, tn=1024, name="matmul"):
    m, k = a.shape
    n = b.shape[1]
    return pl.pallas_call(
        _mm_kernel,
        grid=(m // tm, n // tn),
        in_specs=[pl.BlockSpec((tm, k), lambda i, j: (i, 0)),
                  pl.BlockSpec((k, tn), lambda i, j: (0, j))],
        out_specs=pl.BlockSpec((tm, tn), lambda i, j: (i, j)),
        out_shape=jax.ShapeDtypeStruct((m, n), out_dtype),
        compiler_params=_cparams(("parallel", "parallel"), 56 << 20),
        name=name,
    )(a, b)


def _mm_cast_kernel(a_ref, b_ref, side_ref, o_ref, side_o_ref):
    o_ref[...] = _dot(a_ref[...], b_ref[...]).astype(o_ref.dtype)
    side_o_ref[...] = side_ref[...].astype(side_o_ref.dtype)


def matmul_and_cast(a, b, side, out_dtype, tm=1024, tn=1024, ts=1024, name="matmul_and_cast"):
    m, k = a.shape
    n = b.shape[1]
    gj = n // tn
    side_cols = side.shape[1] // ts
    assert (m // tm) * gj == (side.shape[0] // ts) * side_cols

    def side_map(i, j):
        tile = i * gj + j
        return (tile // side_cols, tile % side_cols)

    return pl.pallas_call(
        _mm_cast_kernel,
        grid=(m // tm, gj),
        in_specs=[pl.BlockSpec((tm, k), lambda i, j: (i, 0)),
                  pl.BlockSpec((k, tn), lambda i, j: (0, j)),
                  pl.BlockSpec((ts, ts), side_map)],
        out_specs=[pl.BlockSpec((tm, tn), lambda i, j: (i, j)),
                   pl.BlockSpec((ts, ts), side_map)],
        out_shape=[jax.ShapeDtypeStruct((m, n), out_dtype),
                   jax.ShapeDtypeStruct(side.shape, BF16)],
        compiler_params=_cparams(("parallel", "parallel"), VMEM_CAP),
        name=name,
    )(a, b, side)


def _rope128(x, c, s):
    lane = lax.broadcasted_iota(jnp.int32, x.shape, 1)
    swapped = jnp.where(lane < QK_ROPE // 2,
                        pltpu.roll(x, LANES - QK_ROPE // 2, 1),
                        pltpu.roll(x, QK_ROPE // 2, 1))
    return x * c + swapped * s


def _small_proj_kernel(a_ref, w_ref, qn_ref, kvn_ref, c_ref, s_ref, cq_ref, ckv_ref, kpe_ref):
    z = _dot(a_ref[...], w_ref[...])
    cq_ref[...] = _rms(z[:, :Q_LORA], qn_ref[...]).astype(cq_ref.dtype)
    ckv_ref[...] = _rms(z[:, Q_LORA:Q_LORA + KV_LORA], kvn_ref[...]).astype(ckv_ref.dtype)
    kr = z[:, Q_LORA + KV_LORA:]
    kpe_ref[...] = _rope128(kr, c_ref[...], s_ref[...]).astype(kpe_ref.dtype)


def small_proj(xn, w_small, q_norm, kv_norm, rope_c, rope_s, seq, tm=512):
    t, d = xn.shape
    nseq = seq // tm
    return pl.pallas_call(
        _small_proj_kernel,
        grid=(t // tm,),
        in_specs=[pl.BlockSpec((tm, d), lambda i: (i, 0)),
                  pl.BlockSpec((d, SMALL_COLS), lambda i: (0, 0)),
                  pl.BlockSpec((1, Q_LORA), lambda i: (0, 0)),
                  pl.BlockSpec((1, KV_LORA), lambda i: (0, 0)),
                  pl.BlockSpec((tm, LANES), lambda i: (i % nseq, 0)),
                  pl.BlockSpec((tm, LANES), lambda i: (i % nseq, 0))],
        out_specs=[pl.BlockSpec((tm, Q_LORA), lambda i: (i, 0)),
                   pl.BlockSpec((tm, KV_LORA), lambda i: (i, 0)),
                   pl.BlockSpec((tm, LANES), lambda i: (i, 0))],
        out_shape=[jax.ShapeDtypeStruct((t, Q_LORA), BF16),
                   jax.ShapeDtypeStruct((t, KV_LORA), BF16),
                   jax.ShapeDtypeStruct((t, LANES), BF16)],
        compiler_params=_cparams(("parallel",), 56 << 20),
        name="small_proj",
    )(xn, w_small, q_norm.reshape(1, -1), kv_norm.reshape(1, -1), rope_c, rope_s)


def _softplus(x):
    return jnp.maximum(x, 0.0) + jnp.log1p(jnp.exp(-jnp.abs(x)))


def _segment_rows(tc):
    n = tc // SUBLANES
    return [j0 * SUBLANES + i for i, j0 in (divmod(m * SUBLANES, n) for m in range(n))]


def _to_segment_major(buf_ref, slot0, src_ref):
    tc, w = src_ref.shape
    for c in range(w // LANES):
        for m, row in enumerate(_segment_rows(tc)):
            buf_ref.at[c][pl.ds(slot0 * SUBLANES + row, SUBLANES, stride=SUBLANES), :] = (
                src_ref[m * SUBLANES:(m + 1) * SUBLANES, c * LANES:(c + 1) * LANES])


def _from_segment_major(buf_ref):
    tc = buf_ref.shape[1]
    return jnp.concatenate(
        [jnp.concatenate([buf_ref.at[c][pl.ds(row, SUBLANES, stride=SUBLANES), :]
                          for row in _segment_rows(tc)], axis=0)
         for c in range(buf_ref.shape[0])], axis=1)


def _conv_segment_major(xbuf_ref, x_ref, prev, nxt, cw, cb):
    tc = x_ref.shape[0]
    n = tc // SUBLANES
    _to_segment_major(xbuf_ref, CONV_LEFT, x_ref)
    sub = lax.broadcasted_iota(jnp.int32, (SUBLANES, LANES), 0)
    out = []
    for c in range(xbuf_ref.shape[0]):
        lanes = slice(c * LANES, (c + 1) * LANES)

        def slot(j):
            return xbuf_ref[c, (j + CONV_LEFT) * SUBLANES:(j + CONV_LEFT + 1) * SUBLANES, :]

        for back in range(1, CONV_LEFT + 1):
            halo = jnp.broadcast_to(prev[SUBLANES - back:SUBLANES - back + 1, lanes], sub.shape)
            xbuf_ref[c, (CONV_LEFT - back) * SUBLANES:(CONV_LEFT - back + 1) * SUBLANES, :] = jnp.where(
                sub == 0, halo, pltpu.roll(slot(n - back), 1, 0))
        for fwd in range(CONV_WIDTH - 1 - CONV_LEFT):
            halo = jnp.broadcast_to(nxt[fwd:fwd + 1, lanes], sub.shape)
            xbuf_ref[c, (n + CONV_LEFT + fwd) * SUBLANES:(n + CONV_LEFT + fwd + 1) * SUBLANES, :] = jnp.where(
                sub == SUBLANES - 1, halo, pltpu.roll(slot(fwd), SUBLANES - 1, 0))
        xc = cb[:, lanes]
        for k in range(CONV_WIDTH):
            xc = xc + xbuf_ref[c, k * SUBLANES:k * SUBLANES + tc, :] * cw[k:k + 1, lanes]
        out.append(xc)
    return jnp.concatenate(out, axis=1)


def _scan_segments(a, u, c_in, reverse):
    n = a.shape[0] // SUBLANES
    local = [None] * n
    prod = [None] * n
    h = p = None
    for j in (range(n - 1, -1, -1) if reverse else range(n)):
        aj = a[j * SUBLANES:(j + 1) * SUBLANES]
        uj = u[j * SUBLANES:(j + 1) * SUBLANES]
        if h is None:
            h, p = uj, aj
        else:
            h, p = aj * h + uj, aj * p
        local[j], prod[j] = h, p
    carries = [None] * SUBLANES
    c = c_in
    for i in (range(SUBLANES - 1, -1, -1) if reverse else range(SUBLANES)):
        carries[i] = c
        c = h[i:i + 1] + p[i:i + 1] * c
    cmat = jnp.concatenate(carries, axis=0)
    out = jnp.concatenate([local[j] + prod[j] * cmat for j in range(n)], axis=0)
    return out, c


def _lru_kernel(x_ref, prev_ref, next_ref, y_ref, cw_ref, cb_ref, wa_ref, ba_ref, wx_ref, bx_ref,
                lam_ref, o_ref, xbuf_ref, obuf_ref, xc_ref, hf_ref, carry_ref, *, n_chunks, tc):
    s = pl.program_id(2)
    rate = (-LRU_C * LOG2E) * _softplus(-lam_ref[...])

    def gates_scan(xc, reverse):
        xcb = xc.astype(BF16)
        r = jax.nn.sigmoid(_dot(xcb, wa_ref[...]) + ba_ref[...])
        i = jax.nn.sigmoid(_dot(xcb, wx_ref[...]) + bx_ref[...])
        a = jnp.exp2(r * rate)
        u = jnp.sqrt(1.0 - a * a) * (i * xc)
        return _scan_segments(a, u, carry_ref[0:1], reverse)

    @pl.when(jnp.logical_or(s == 0, s == n_chunks))
    def _():
        carry_ref[...] = jnp.zeros_like(carry_ref)

    @pl.when(s < n_chunks)
    def _():
        prev = jnp.where(s == 0, 0.0, prev_ref[...])
        nxt = jnp.where(s == n_chunks - 1, 0.0, next_ref[...])
        xc = _conv_segment_major(xbuf_ref, x_ref, prev, nxt, cw_ref[...], cb_ref[...])
        h, c = gates_scan(xc, reverse=False)
        row0 = pl.multiple_of(s * tc, tc)
        xc_ref[pl.ds(row0, tc), :] = xc
        hf_ref[pl.ds(row0, tc), :] = h
        carry_ref[0:1] = c

    @pl.when(s >= n_chunks)
    def _():
        row0 = pl.multiple_of((2 * n_chunks - 1 - s) * tc, tc)
        h, c = gates_scan(xc_ref[pl.ds(row0, tc), :], reverse=True)
        carry_ref[0:1] = c
        tot = hf_ref[pl.ds(row0, tc), :] + h
        for half in range(obuf_ref.shape[0]):
            obuf_ref[half] = tot[:, half * LANES:(half + 1) * LANES]
        o_ref[...] = (_from_segment_major(obuf_ref) * jax.nn.gelu(y_ref[...])).astype(o_ref.dtype)


def lru_branch(z_lru, conv_w, conv_b, wa, ba, wx, bx, lam, batch, seq, tc=1024):
    t = z_lru.shape[0]
    w = LRU_BLOCK
    n_chunks = seq // tc
    y_col0 = D_MODEL // w
    last = n_chunks - 1

    def fwd_ix(s):
        return jnp.minimum(s, last)

    def bwd_ix(s):
        return jnp.where(s >= n_chunks, 2 * n_chunks - 1 - s, last)

    def x_map(b, h, s):
        return (b * n_chunks + fwd_ix(s), h)

    def prev_map(b, h, s):
        return (jnp.maximum((b * seq + fwd_ix(s) * tc) // SUBLANES - 1, 0), h)

    def next_map(b, h, s):
        return (jnp.minimum((b * seq + (fwd_ix(s) + 1) * tc) // SUBLANES, t // SUBLANES - 1), h)

    def y_map(b, h, s):
        return (b * n_chunks + bwd_ix(s), y_col0 + h)

    def o_map(b, h, s):
        return (b * n_chunks + bwd_ix(s), h)

    def dir_map(b, h, s):
        return (s // n_chunks, h, 0, 0)

    return pl.pallas_call(
        functools.partial(_lru_kernel, n_chunks=n_chunks, tc=tc),
        grid=(batch, LRU_HEADS, 2 * n_chunks),
        in_specs=[pl.BlockSpec((tc, w), x_map),
                  pl.BlockSpec((SUBLANES, w), prev_map),
                  pl.BlockSpec((SUBLANES, w), next_map),
                  pl.BlockSpec((tc, w), y_map),
                  pl.BlockSpec((CONV_WIDTH, w), lambda b, h, s: (0, h)),
                  pl.BlockSpec((1, w), lambda b, h, s: (0, h)),
                  pl.BlockSpec((None, None, w, w), dir_map),
                  pl.BlockSpec((None, None, 1, w), dir_map),
                  pl.BlockSpec((None, None, w, w), dir_map),
                  pl.BlockSpec((None, None, 1, w), dir_map),
                  pl.BlockSpec((None, None, 1, w), dir_map)],
        out_specs=pl.BlockSpec((tc, w), o_map),
        out_shape=jax.ShapeDtypeStruct((t, D_MODEL), BF16),
        scratch_shapes=[pltpu.VMEM((w // LANES, tc + (CONV_WIDTH - 1) * SUBLANES, LANES), F32),
                        pltpu.VMEM((w // LANES, tc, LANES), F32),
                        pltpu.VMEM((seq, w), F32),
                        pltpu.VMEM((seq, w), F32),
                        pltpu.VMEM((SUBLANES, w), F32)],
        compiler_params=_cparams(("parallel", "parallel", "arbitrary"), 48 << 20),
        name="lru_branch",
    )(z_lru, z_lru, z_lru, z_lru, conv_w, conv_b.reshape(1, -1), wa, ba, wx, bx, lam)


def _q_up_kernel(a_ref, w_ref, c_ref, s_ref, o_ref, *, heads):
    q = _dot(a_ref[...], w_ref[...])
    c = c_ref[...]
    s = s_ref[...]
    for h in range(heads):
        lo = h * QK_PAD
        o_ref[:, lo:lo + LANES] = q[:, lo:lo + LANES].astype(o_ref.dtype)
        o_ref[:, lo + LANES:lo + QK_PAD] = _rope128(q[:, lo + LANES:lo + QK_PAD], c, s).astype(o_ref.dtype)


def q_up(cq, w_q, rope_c, rope_s, seq, tm=1024, heads=8):
    t, k = cq.shape
    n = w_q.shape[1]
    tn = heads * QK_PAD
    nseq = seq // tm
    return pl.pallas_call(
        functools.partial(_q_up_kernel, heads=heads),
        grid=(t // tm, n // tn),
        in_specs=[pl.BlockSpec((tm, k), lambda i, j: (i, 0)),
                  pl.BlockSpec((k, tn), lambda i, j: (0, j)),
                  pl.BlockSpec((tm, LANES), lambda i, j: (i % nseq, 0)),
                  pl.BlockSpec((tm, LANES), lambda i, j: (i % nseq, 0))],
        out_specs=pl.BlockSpec((tm, tn), lambda i, j: (i, j)),
        out_shape=jax.ShapeDtypeStruct((t, n), BF16),
        compiler_params=_cparams(("parallel", "parallel"), 40 << 20),
        name="q_up",
    )(cq, w_q, rope_c, rope_s)


def _kv_up_kernel(a_ref, wk_ref, wv_ref, kpe_ref, k_ref, v_ref, *, heads):
    a = a_ref[...]
    kn = _dot(a, wk_ref[...])
    vv = _dot(a, wv_ref[...])
    kpe = kpe_ref[...]
    ones = jnp.ones((a.shape[0], V_PAD - V_HEAD), v_ref.dtype)
    for h in range(heads):
        k_ref[:, h * QK_PAD:h * QK_PAD + LANES] = kn[:, h * LANES:(h + 1) * LANES].astype(k_ref.dtype)
        k_ref[:, h * QK_PAD + LANES:(h + 1) * QK_PAD] = kpe
        v_ref[:, h * V_PAD:h * V_PAD + V_HEAD] = vv[:, h * V_HEAD:(h + 1) * V_HEAD].astype(v_ref.dtype)
        v_ref[:, h * V_PAD + V_HEAD:(h + 1) * V_PAD] = ones


def kv_up(ckv, w_k, w_v, kpe, tm=1024, heads=8):
    t, k = ckv.shape
    n = w_k.shape[1]
    tn = heads * LANES
    return pl.pallas_call(
        functools.partial(_kv_up_kernel, heads=heads),
        grid=(t // tm, n // tn),
        in_specs=[pl.BlockSpec((tm, k), lambda i, j: (i, 0)),
                  pl.BlockSpec((k, tn), lambda i, j: (0, j)),
                  pl.BlockSpec((k, tn), lambda i, j: (0, j)),
                  pl.BlockSpec((tm, LANES), lambda i, j: (i, 0))],
        out_specs=[pl.BlockSpec((tm, heads * QK_PAD), lambda i, j: (i, j)),
                   pl.BlockSpec((tm, heads * V_PAD), lambda i, j: (i, j))],
        out_shape=[jax.ShapeDtypeStruct((t, MLA_HEADS * QK_PAD), BF16),
                   jax.ShapeDtypeStruct((t, MLA_HEADS * V_PAD), BF16)],
        compiler_params=_cparams(("parallel", "parallel"), 40 << 20),
        name="kv_up",
    )(ckv, w_k, w_v, kpe)


def _attn_kernel(q_ref, k_ref, v_ref, o_ref, *, tk):
    q = q_ref[...]
    nk = k_ref.shape[0] // tk
    c = QK_HEAD ** -0.5 * LOG2E

    def scores(j):
        return lax.dot_general(q, k_ref[j * tk:(j + 1) * tk, :], (((1,), (1,)), ((), ())),
                               preferred_element_type=F32)

    s_next = scores(0)
    m = acc = None
    for j in range(nk):
        s = s_next
        if j + 1 < nk:
            s_next = scores(j + 1)
        m_new = jnp.max(s, axis=-1, keepdims=True)
        if j > 0:
            m_new = jnp.maximum(m, m_new)
        p = jnp.exp2((s - m_new) * c).astype(BF16)
        pv = _dot(p, v_ref[j * tk:(j + 1) * tk, :])
        acc = pv if j == 0 else jnp.exp2((m - m_new) * c) * acc + pv
        m = m_new
    o_ref[...] = (acc[:, :V_HEAD] / acc[:, V_HEAD:]).astype(o_ref.dtype)


def attention(q, k, v, batch, seq, tq=1024, tk=1024):
    t = q.shape[0]
    nq = seq // tq
    return pl.pallas_call(
        functools.partial(_attn_kernel, tk=tk),
        grid=(batch, MLA_HEADS, nq),
        in_specs=[pl.BlockSpec((tq, QK_PAD), lambda b, h, i: (b * nq + i, h)),
                  pl.BlockSpec((seq, QK_PAD), lambda b, h, i: (b, h)),
                  pl.BlockSpec((seq, V_PAD), lambda b, h, i: (b, h))],
        out_specs=pl.BlockSpec((tq, V_HEAD), lambda b, h, i: (b * nq + i, h)),
        out_shape=jax.ShapeDtypeStruct((t, MLA_HEADS * V_HEAD), BF16),
        compiler_params=_cparams(("parallel", "parallel", "arbitrary"), 48 << 20),
        name="attention",
    )(q, k, v)


def _merge_kernel(al_ref, wl_ref, am_ref, wm_ref, ga_ref, gb_ref, o_ref):
    o_lru = _dot(al_ref[...], wl_ref[...])
    o_mla = _dot(am_ref[...], wm_ref[...])
    merged = jax.nn.sigmoid(ga_ref[...]) * o_lru + jax.nn.sigmoid(gb_ref[...]) * o_mla
    o_ref[...] = merged.astype(o_ref.dtype)


def merge_proj(a_lru, w_lru, attn, w_mla, z_gate, tm=1024, tn=256):
    m, k = attn.shape
    n = w_mla.shape[1]
    gb0 = D_MODEL // tn
    return pl.pallas_call(
        _merge_kernel,
        grid=(m // tm, n // tn),
        in_specs=[pl.BlockSpec((tm, k), lambda i, j: (i, 0)),
                  pl.BlockSpec((k, tn), lambda i, j: (0, j)),
                  pl.BlockSpec((tm, k), lambda i, j: (i, 0)),
                  pl.BlockSpec((k, tn), lambda i, j: (0, j)),
                  pl.BlockSpec((tm, tn), lambda i, j: (i, j)),
                  pl.BlockSpec((tm, tn), lambda i, j: (i, gb0 + j))],
        out_specs=pl.BlockSpec((tm, tn), lambda i, j: (i, j)),
        out_shape=jax.ShapeDtypeStruct((m, n), BF16),
        compiler_params=_cparams(("parallel", "parallel"), 56 << 20),
        name="merge_proj",
    )(a_lru, w_lru, attn, w_mla, z_gate, z_gate)


def _mm_res_kernel(a_ref, w_ref, r_ref, o_ref):
    o_ref[...] = r_ref[...] + _dot(a_ref[...], w_ref[...])


def matmul_residual(a, w, res, tm=1024, tn=1024):
    m, k = a.shape
    n = w.shape[1]
    return pl.pallas_call(
        _mm_res_kernel,
        grid=(m // tm, n // tn),
        in_specs=[pl.BlockSpec((tm, k), lambda i, j: (i, 0)),
                  pl.BlockSpec((k, tn), lambda i, j: (0, j)),
                  pl.BlockSpec((tm, tn), lambda i, j: (i, j))],
        out_specs=pl.BlockSpec((tm, tn), lambda i, j: (i, j)),
        out_shape=jax.ShapeDtypeStruct((m, n), F32),
        compiler_params=_cparams(("parallel", "parallel"), 56 << 20),
        name="matmul_residual",
    )(a, w, res)


def _up_kernel(a_ref, w_ref, o_ref):
    o_ref[...] = jnp.square(jnp.maximum(_dot(a_ref[...], w_ref[...]), 0.0)).astype(o_ref.dtype)


def mlp_up(a, w, tm=1024, tn=1024):
    m, k = a.shape
    n = w.shape[1]
    return pl.pallas_call(
        _up_kernel,
        grid=(m // tm, n // tn),
        in_specs=[pl.BlockSpec((tm, k), lambda i, j: (i, 0)),
                  pl.BlockSpec((k, tn), lambda i, j: (0, j))],
        out_specs=pl.BlockSpec((tm, tn), lambda i, j: (i, j)),
        out_shape=jax.ShapeDtypeStruct((m, n), BF16),
        compiler_params=_cparams(("parallel", "parallel"), 56 << 20),
        name="mlp_up",
    )(a, w)


def _down_kernel(a_ref, w_ref, r_ref, o_ref):
    kk = pl.program_id(2)

    @pl.when(kk == 0)
    def _():
        o_ref[...] = r_ref[...] + _dot(a_ref[...], w_ref[...])

    @pl.when(kk > 0)
    def _():
        o_ref[...] += _dot(a_ref[...], w_ref[...])


def mlp_down(a, w, res, tm=1024, tn=1024, tk=4096):
    m, k = a.shape
    n = w.shape[1]
    return pl.pallas_call(
        _down_kernel,
        grid=(m // tm, n // tn, k // tk),
        in_specs=[pl.BlockSpec((tm, tk), lambda i, j, kk: (i, kk)),
                  pl.BlockSpec((tk, tn), lambda i, j, kk: (kk, j)),
                  pl.BlockSpec((tm, tn), lambda i, j, kk: (i, j))],
        out_specs=pl.BlockSpec((tm, tn), lambda i, j, kk: (i, j)),
        out_shape=jax.ShapeDtypeStruct((m, n), F32),
        compiler_params=_cparams(("parallel", "parallel", "arbitrary"), 56 << 20),
        name="mlp_down",
    )(a, w, res)


def _rope_tables(seq):
    inv = 1.0 / (ROPE_THETA ** (jnp.arange(0, QK_ROPE, 2, dtype=F32) / QK_ROPE))
    ang = jnp.arange(seq, dtype=F32)[:, None] * inv[None, :]
    cos, sin = jnp.cos(ang), jnp.sin(ang)
    zero = jnp.zeros((seq, LANES - QK_ROPE), F32)
    return (jnp.concatenate([cos, cos, zero], axis=1),
            jnp.concatenate([-sin, sin, zero], axis=1))


def _prep_weights(w_in, lru_wa, lru_ba, lru_wx, lru_bx, lru_lam, w_q_up, w_kv_up,
                  w_lru_proj, w_mla_proj, w_out):
    d = D_MODEL
    small_lo, small_hi = 2 * d, 2 * d + Q_LORA + KV_LORA + QK_ROPE
    small = w_in[:, small_lo:small_hi]
    w_small = jnp.pad(small, ((0, 0), (0, SMALL_COLS - small.shape[1]))).astype(BF16)
    wq = w_q_up.reshape(Q_LORA, MLA_HEADS, QK_HEAD)
    wq = jnp.pad(wq, ((0, 0), (0, 0), (0, QK_PAD - QK_HEAD))).reshape(Q_LORA, MLA_HEADS * QK_PAD)
    wkv = w_kv_up.reshape(KV_LORA, MLA_HEADS, QK_NOPE + V_HEAD)
    wk = wkv[:, :, :QK_NOPE].reshape(KV_LORA, MLA_HEADS * QK_NOPE)
    wv = wkv[:, :, QK_NOPE:].reshape(KV_LORA, MLA_HEADS * V_HEAD)
    shp = (2, LRU_HEADS, 1, LRU_BLOCK)
    return dict(
        w_lru_in=w_in[:, :small_lo].astype(BF16), w_gate_in=w_in[:, small_hi:].astype(BF16),
        w_small=w_small, wq=wq.astype(BF16), wk=wk.astype(BF16), wv=wv.astype(BF16),
        wa=lru_wa.astype(BF16), wx=lru_wx.astype(BF16),
        ba=lru_ba.reshape(shp), bx=lru_bx.reshape(shp), lam=lru_lam.reshape(shp),
        w_lru_proj=w_lru_proj.astype(BF16), w_mla_proj=w_mla_proj.astype(BF16),
        w_out=w_out.astype(BF16))


def _proj_and_cast(xn, w, side, name, tile=1024):
    steps = (xn.shape[0] // tile) * (w.shape[1] // tile)
    if steps == (side.shape[0] // tile) * (side.shape[1] // tile):
        return matmul_and_cast(xn, w, side, F32, tm=tile, tn=tile, ts=tile, name=name)
    return matmul(xn, w, F32, tm=tile, tn=tile, name=name), side.astype(BF16)


def _layer(x3, p, norm1, conv_w, conv_b, q_norm, kv_norm, norm2, norm_f, w_up, w_down):
    batch, seq, d = x3.shape
    x = x3.reshape(batch * seq, d)
    rope_c, rope_s = _rope_tables(seq)

    xn = rmsnorm(x, norm1, BF16)
    if w_up.dtype == BF16:
        z_lru = matmul(xn, p["w_lru_in"], F32, name="in_proj_lru")
        z_gate = matmul(xn, p["w_gate_in"], F32, name="in_proj_gate")
    else:
        z_lru, w_up = _proj_and_cast(xn, p["w_lru_in"], w_up, "in_proj_lru")
        z_gate, w_down = _proj_and_cast(xn, p["w_gate_in"], w_down, "in_proj_gate")
    cq, ckv, kpe = small_proj(xn, p["w_small"], q_norm, kv_norm, rope_c, rope_s, seq)

    a_lru = lru_branch(z_lru, conv_w, conv_b, p["wa"], p["ba"], p["wx"], p["bx"], p["lam"], batch, seq)

    q = q_up(cq, p["wq"], rope_c, rope_s, seq)
    k, v = kv_up(ckv, p["wk"], p["wv"], kpe)
    attn = attention(q, k, v, batch, seq, tk=min(2048, seq // 2))

    merged = merge_proj(a_lru, p["w_lru_proj"], attn, p["w_mla_proj"], z_gate)
    h = matmul_residual(merged, p["w_out"], x)

    hn = rmsnorm(h, norm2, BF16)
    u = mlp_up(hn, w_up)
    h2 = mlp_down(u, w_down, h)
    return rmsnorm(h2, norm_f, F32).reshape(batch, seq, d), w_up, w_down


def kernel(x_prompt, x_sample, norm1, w_in, conv_w, conv_b, lru_wa, lru_ba, lru_wx, lru_bx, lru_lam,
           q_norm, w_q_up, kv_norm, w_kv_up, w_lru_proj, w_mla_proj, w_out, norm2, w_up, w_down, norm_f):
    assert norm1.shape[0] == 1, "single-layer trunk"
    p = _prep_weights(w_in[0], lru_wa[0], lru_ba[0], lru_wx[0], lru_bx[0], lru_lam[0], w_q_up[0],
                      w_kv_up[0], w_lru_proj[0], w_mla_proj[0], w_out[0])
    args = (p, norm1[0], conv_w[0], conv_b[0], q_norm[0], kv_norm[0], norm2[0], norm_f)
    y_prompt, w_up_bf16, w_down_bf16 = _layer(x_prompt, *args, w_up[0], w_down[0])
    y_sample, _, _ = _layer(x_sample, *args, w_up_bf16, w_down_bf16)
    return (y_prompt, y_sample)
```

```python
import functools
import math

import jax
import jax.numpy as jnp
from jax import lax
from jax.experimental import pallas as pl
from jax.experimental.pallas import tpu as pltpu

D_MODEL = 4096
LRU_HEADS = 16
LRU_BLOCK = 256
CONV_WIDTH = 4
CONV_LEFT = 2
LRU_C = 8.0
MLA_HEADS = 32
Q_LORA = 1024
KV_LORA = 512
QK_NOPE = 128
QK_ROPE = 64
V_HEAD = 128
QK_HEAD = QK_NOPE + QK_ROPE
ROPE_THETA = 10000.0
D_FF = 4 * D_MODEL
EPS = 1e-6
LOG2E = math.log2(math.e)

LANES = 128
SUBLANES = 8
QK_PAD = 2 * LANES
V_PAD = 2 * LANES
SMALL_COLS = 13 * LANES
VMEM_CAP = 60 * 1024 * 1024

BF16 = jnp.bfloat16
F32 = jnp.float32


def _cparams(semantics, vmem_bytes):
    return pltpu.CompilerParams(dimension_semantics=semantics,
                                vmem_limit_bytes=min(int(vmem_bytes), VMEM_CAP))


def _dot(a, b):
    return jnp.dot(a, b, preferred_element_type=F32)


def _rms(x, g):
    y = x * lax.rsqrt(jnp.mean(jnp.square(x), axis=-1, keepdims=True) + EPS)
    return y * g


def _rmsnorm_kernel(x_ref, g_ref, o_ref):
    o_ref[...] = _rms(x_ref[...], g_ref[...]).astype(o_ref.dtype)


def rmsnorm(x, g, out_dtype, tr=256):
    t, d = x.shape
    return pl.pallas_call(
        _rmsnorm_kernel,
        grid=(t // tr,),
        in_specs=[pl.BlockSpec((tr, d), lambda i: (i, 0)),
                  pl.BlockSpec((1, d), lambda i: (0, 0))],
        out_specs=pl.BlockSpec((tr, d), lambda i: (i, 0)),
        out_shape=jax.ShapeDtypeStruct((t, d), out_dtype),
        compiler_params=_cparams(("parallel",), 40 << 20),
        name="rmsnorm",
    )(x, g.reshape(1, d))


def _mm_kernel(a_ref, b_ref, o_ref):
    o_ref[...] = _dot(a_ref[...], b_ref[...]).astype(o_ref.dtype)


def matmul(a, b, out_dtype, tm=1024, tn=1024, name="matmul"):
    m, k = a.shape
    n = b.shape[1]
    return pl.pallas_call(
        _mm_kernel,
        grid=(m // tm, n // tn),
        in_specs=[pl.BlockSpec((tm, k), lambda i, j: (i, 0)),
                  pl.BlockSpec((k, tn), lambda i, j: (0, j))],
        out_specs=pl.BlockSpec((tm, tn), lambda i, j: (i, j)),
        out_shape=jax.ShapeDtypeStruct((m, n), out_dtype),
        compiler_params=_cparams(("parallel", "parallel"), 56 << 20),
        name=name,
    )(a, b)


def _mm_cast_kernel(a_ref, b_ref, side_ref, o_ref, side_o_ref):
    o_ref[...] = _dot(a_ref[...], b_ref[...]).astype(o_ref.dtype)
    side_o_ref[...] = side_ref[...].astype(side_o_ref.dtype)


def matmul_and_cast(a, b, side, out_dtype, tm=1024, tn=1024, ts=1024, name="matmul_and_cast"):
    m, k = a.shape
    n = b.shape[1]
    gj = n // tn
    side_cols = side.shape[1] // ts
    assert (m // tm) * gj == (side.shape[0] // ts) * side_cols

    def side_map(i, j):
        tile = i * gj + j
        return (tile // side_cols, tile % side_cols)

    return pl.pallas_call(
        _mm_cast_kernel,
        grid=(m // tm, gj),
        in_specs=[pl.BlockSpec((tm, k), lambda i, j: (i, 0)),
                  pl.BlockSpec((k, tn), lambda i, j: (0, j)),
                  pl.BlockSpec((ts, ts), side_map)],
        out_specs=[pl.BlockSpec((tm, tn), lambda i, j: (i, j)),
                   pl.BlockSpec((ts, ts), side_map)],
        out_shape=[jax.ShapeDtypeStruct((m, n), out_dtype),
                   jax.ShapeDtypeStruct(side.shape, BF16)],
        compiler_params=_cparams(("parallel", "parallel"), VMEM_CAP),
        name=name,
    )(a, b, side)


def _mm_norm_kernel(a_ref, b_ref, x_ref, g_ref, o_ref, xo_ref, *, relu2):
    z = _dot(a_ref[...], b_ref[...])
    if relu2:
        z = jnp.square(jnp.maximum(z, 0.0))
    o_ref[...] = z.astype(o_ref.dtype)
    xo_ref[...] = _rms(x_ref[...], g_ref[...]).astype(xo_ref.dtype)


def matmul_and_norm(a, b, x, g, out_dtype, norm_dtype, relu2=False, tm=1024, tn=1024, name="matmul_and_norm"):
    m, k = a.shape
    n = b.shape[1]
    gj = n // tn
    rows = x.shape[0] // ((m // tm) * gj)
    d = x.shape[1]
    return pl.pallas_call(
        functools.partial(_mm_norm_kernel, relu2=relu2),
        grid=(m // tm, gj),
        in_specs=[pl.BlockSpec((tm, k), lambda i, j: (i, 0)),
                  pl.BlockSpec((k, tn), lambda i, j: (0, j)),
                  pl.BlockSpec((rows, d), lambda i, j: (i * gj + j, 0)),
                  pl.BlockSpec((1, d), lambda i, j: (0, 0))],
        out_specs=[pl.BlockSpec((tm, tn), lambda i, j: (i, j)),
                   pl.BlockSpec((rows, d), lambda i, j: (i * gj + j, 0))],
        out_shape=[jax.ShapeDtypeStruct((m, n), out_dtype),
                   jax.ShapeDtypeStruct(x.shape, norm_dtype)],
        compiler_params=_cparams(("parallel", "parallel"), 56 << 20),
        name=name,
    )(a, b, x, g.reshape(1, d))


def _norm_can_ride(m, n, x_rows, tile=1024, row_align=16):
    steps = (m // tile) * (n // tile)
    return x_rows % steps == 0 and (x_rows // steps) % row_align == 0


def _rope128(x, c, s):
    lane = lax.broadcasted_iota(jnp.int32, x.shape, 1)
    swapped = jnp.where(lane < QK_ROPE // 2,
                        pltpu.roll(x, LANES - QK_ROPE // 2, 1),
                        pltpu.roll(x, QK_ROPE // 2, 1))
    return x * c + swapped * s


def _small_proj_kernel(a_ref, w_ref, qn_ref, kvn_ref, c_ref, s_ref, cq_ref, ckv_ref, kpe_ref):
    z = _dot(a_ref[...], w_ref[...])
    cq_ref[...] = _rms(z[:, :Q_LORA], qn_ref[...]).astype(cq_ref.dtype)
    ckv_ref[...] = _rms(z[:, Q_LORA:Q_LORA + KV_LORA], kvn_ref[...]).astype(ckv_ref.dtype)
    kr = z[:, Q_LORA + KV_LORA:]
    kpe_ref[...] = _rope128(kr, c_ref[...], s_ref[...]).astype(kpe_ref.dtype)


def small_proj(xn, w_small, q_norm, kv_norm, rope_c, rope_s, seq, tm=512):
    t, d = xn.shape
    nseq = seq // tm
    return pl.pallas_call(
        _small_proj_kernel,
        grid=(t // tm,),
        in_specs=[pl.BlockSpec((tm, d), lambda i: (i, 0)),
                  pl.BlockSpec((d, SMALL_COLS), lambda i: (0, 0)),
                  pl.BlockSpec((1, Q_LORA), lambda i: (0, 0)),
                  pl.BlockSpec((1, KV_LORA), lambda i: (0, 0)),
                  pl.BlockSpec((tm, LANES), lambda i: (i % nseq, 0)),
                  pl.BlockSpec((tm, LANES), lambda i: (i % nseq, 0))],
        out_specs=[pl.BlockSpec((tm, Q_LORA), lambda i: (i, 0)),
                   pl.BlockSpec((tm, KV_LORA), lambda i: (i, 0)),
                   pl.BlockSpec((tm, LANES), lambda i: (i, 0))],
        out_shape=[jax.ShapeDtypeStruct((t, Q_LORA), BF16),
                   jax.ShapeDtypeStruct((t, KV_LORA), BF16),
                   jax.ShapeDtypeStruct((t, LANES), BF16)],
        compiler_params=_cparams(("parallel",), 56 << 20),
        name="small_proj",
    )(xn, w_small, q_norm.reshape(1, -1), kv_norm.reshape(1, -1), rope_c, rope_s)


def _softplus(x):
    return jnp.maximum(x, 0.0) + jnp.log1p(jnp.exp(-jnp.abs(x)))


def _segment_rows(tc):
    n = tc // SUBLANES
    return [j0 * SUBLANES + i for i, j0 in (divmod(m * SUBLANES, n) for m in range(n))]


def _to_segment_major(buf_ref, slot0, src_ref):
    tc, w = src_ref.shape
    for c in range(w // LANES):
        for m, row in enumerate(_segment_rows(tc)):
            buf_ref.at[c][pl.ds(slot0 * SUBLANES + row, SUBLANES, stride=SUBLANES), :] = (
                src_ref[m * SUBLANES:(m + 1) * SUBLANES, c * LANES:(c + 1) * LANES])


def _from_segment_major(buf_ref):
    tc = buf_ref.shape[1]
    return jnp.concatenate(
        [jnp.concatenate([buf_ref.at[c][pl.ds(row, SUBLANES, stride=SUBLANES), :]
                          for row in _segment_rows(tc)], axis=0)
         for c in range(buf_ref.shape[0])], axis=1)


def _conv_segment_major(xbuf_ref, x_ref, prev, nxt, cw, cb):
    tc = x_ref.shape[0]
    n = tc // SUBLANES
    _to_segment_major(xbuf_ref, CONV_LEFT, x_ref)
    sub = lax.broadcasted_iota(jnp.int32, (SUBLANES, LANES), 0)
    out = []
    for c in range(xbuf_ref.shape[0]):
        lanes = slice(c * LANES, (c + 1) * LANES)

        def slot(j):
            return xbuf_ref[c, (j + CONV_LEFT) * SUBLANES:(j + CONV_LEFT + 1) * SUBLANES, :]

        for back in range(1, CONV_LEFT + 1):
            halo = jnp.broadcast_to(prev[SUBLANES - back:SUBLANES - back + 1, lanes], sub.shape)
            xbuf_ref[c, (CONV_LEFT - back) * SUBLANES:(CONV_LEFT - back + 1) * SUBLANES, :] = jnp.where(
                sub == 0, halo, pltpu.roll(slot(n - back), 1, 0))
        for fwd in range(CONV_WIDTH - 1 - CONV_LEFT):
            halo = jnp.broadcast_to(nxt[fwd:fwd + 1, lanes], sub.shape)
            xbuf_ref[c, (n + CONV_LEFT + fwd) * SUBLANES:(n + CONV_LEFT + fwd + 1) * SUBLANES, :] = jnp.where(
                sub == SUBLANES - 1, halo, pltpu.roll(slot(fwd), SUBLANES - 1, 0))
        xc = cb[:, lanes]
        for k in range(CONV_WIDTH):
            xc = xc + xbuf_ref[c, k * SUBLANES:k * SUBLANES + tc, :] * cw[k:k + 1, lanes]
        out.append(xc)
    return jnp.concatenate(out, axis=1)


def _scan_segments(a, u, c_in, reverse):
    n = a.shape[0] // SUBLANES
    local = [None] * n
    prod = [None] * n
    h = p = None
    for j in (range(n - 1, -1, -1) if reverse else range(n)):
        aj = a[j * SUBLANES:(j + 1) * SUBLANES]
        uj = u[j * SUBLANES:(j + 1) * SUBLANES]
        if h is None:
            h, p = uj, aj
        else:
            h, p = aj * h + uj, aj * p
        local[j], prod[j] = h, p
    carries = [None] * SUBLANES
    c = c_in
    for i in (range(SUBLANES - 1, -1, -1) if reverse else range(SUBLANES)):
        carries[i] = c
        c = h[i:i + 1] + p[i:i + 1] * c
    cmat = jnp.concatenate(carries, axis=0)
    out = jnp.concatenate([local[j] + prod[j] * cmat for j in range(n)], axis=0)
    return out, c


def _lru_kernel(x_ref, prev_ref, next_ref, y_ref, cw_ref, cb_ref, wa_ref, ba_ref, wx_ref, bx_ref,
                lam_ref, o_ref, xbuf_ref, obuf_ref, xc_ref, hf_ref, carry_ref, *, n_chunks, tc):
    s = pl.program_id(2)
    rate = (-LRU_C * LOG2E) * _softplus(-lam_ref[...])

    def gates_scan(xc, reverse):
        xcb = xc.astype(BF16)
        r = jax.nn.sigmoid(_dot(xcb, wa_ref[...]) + ba_ref[...])
        i = jax.nn.sigmoid(_dot(xcb, wx_ref[...]) + bx_ref[...])
        a = jnp.exp2(r * rate)
        u = jnp.sqrt(1.0 - a * a) * (i * xc)
        return _scan_segments(a, u, carry_ref[0:1], reverse)

    @pl.when(jnp.logical_or(s == 0, s == n_chunks))
    def _():
        carry_ref[...] = jnp.zeros_like(carry_ref)

    @pl.when(s < n_chunks)
    def _():
        prev = jnp.where(s == 0, 0.0, prev_ref[...])
        nxt = jnp.where(s == n_chunks - 1, 0.0, next_ref[...])
        xc = _conv_segment_major(xbuf_ref, x_ref, prev, nxt, cw_ref[...], cb_ref[...])
        h, c = gates_scan(xc, reverse=False)
        row0 = pl.multiple_of(s * tc, tc)
        xc_ref[pl.ds(row0, tc), :] = xc
        hf_ref[pl.ds(row0, tc), :] = h
        carry_ref[0:1] = c

    @pl.when(s >= n_chunks)
    def _():
        row0 = pl.multiple_of((2 * n_chunks - 1 - s) * tc, tc)
        h, c = gates_scan(xc_ref[pl.ds(row0, tc), :], reverse=True)
        carry_ref[0:1] = c
        tot = hf_ref[pl.ds(row0, tc), :] + h
        for half in range(obuf_ref.shape[0]):
            obuf_ref[half] = tot[:, half * LANES:(half + 1) * LANES]
        o_ref[...] = (_from_segment_major(obuf_ref) * jax.nn.gelu(y_ref[...])).astype(o_ref.dtype)


def lru_branch(z_lru, conv_w, conv_b, wa, ba, wx, bx, lam, batch, seq, tc=1024):
    t = z_lru.shape[0]
    w = LRU_BLOCK
    n_chunks = seq // tc
    y_col0 = D_MODEL // w
    last = n_chunks - 1

    def fwd_ix(s):
        return jnp.minimum(s, last)

    def bwd_ix(s):
        return jnp.where(s >= n_chunks, 2 * n_chunks - 1 - s, last)

    def x_map(b, h, s):
        return (b * n_chunks + fwd_ix(s), h)

    def prev_map(b, h, s):
        return (jnp.maximum((b * seq + fwd_ix(s) * tc) // SUBLANES - 1, 0), h)

    def next_map(b, h, s):
        return (jnp.minimum((b * seq + (fwd_ix(s) + 1) * tc) // SUBLANES, t // SUBLANES - 1), h)

    def y_map(b, h, s):
        return (b * n_chunks + bwd_ix(s), y_col0 + h)

    def o_map(b, h, s):
        return (b * n_chunks + bwd_ix(s), h)

    def dir_map(b, h, s):
        return (s // n_chunks, h, 0, 0)

    return pl.pallas_call(
        functools.partial(_lru_kernel, n_chunks=n_chunks, tc=tc),
        grid=(batch, LRU_HEADS, 2 * n_chunks),
        in_specs=[pl.BlockSpec((tc, w), x_map),
                  pl.BlockSpec((SUBLANES, w), prev_map),
                  pl.BlockSpec((SUBLANES, w), next_map),
                  pl.BlockSpec((tc, w), y_map),
                  pl.BlockSpec((CONV_WIDTH, w), lambda b, h, s: (0, h)),
                  pl.BlockSpec((1, w), lambda b, h, s: (0, h)),
                  pl.BlockSpec((None, None, w, w), dir_map),
                  pl.BlockSpec((None, None, 1, w), dir_map),
                  pl.BlockSpec((None, None, w, w), dir_map),
                  pl.BlockSpec((None, None, 1, w), dir_map),
                  pl.BlockSpec((None, None, 1, w), dir_map)],
        out_specs=pl.BlockSpec((tc, w), o_map),
        out_shape=jax.ShapeDtypeStruct((t, D_MODEL), BF16),
        scratch_shapes=[pltpu.VMEM((w // LANES, tc + (CONV_WIDTH - 1) * SUBLANES, LANES), F32),
                        pltpu.VMEM((w // LANES, tc, LANES), F32),
                        pltpu.VMEM((seq, w), F32),
                        pltpu.VMEM((seq, w), F32),
                        pltpu.VMEM((SUBLANES, w), F32)],
        compiler_params=_cparams(("parallel", "parallel", "arbitrary"), 48 << 20),
        name="lru_branch",
    )(z_lru, z_lru, z_lru, z_lru, conv_w, conv_b.reshape(1, -1), wa, ba, wx, bx, lam)


def _q_up_kernel(a_ref, w_ref, c_ref, s_ref, o_ref, *, heads):
    q = _dot(a_ref[...], w_ref[...])
    c = c_ref[...]
    s = s_ref[...]
    for h in range(heads):
        lo = h * QK_PAD
        o_ref[:, lo:lo + LANES] = q[:, lo:lo + LANES].astype(o_ref.dtype)
        o_ref[:, lo + LANES:lo + QK_PAD] = _rope128(q[:, lo + LANES:lo + QK_PAD], c, s).astype(o_ref.dtype)


def q_up(cq, w_q, rope_c, rope_s, seq, tm=1024, heads=8):
    t, k = cq.shape
    n = w_q.shape[1]
    tn = heads * QK_PAD
    nseq = seq // tm
    return pl.pallas_call(
        functools.partial(_q_up_kernel, heads=heads),
        grid=(t // tm, n // tn),
        in_specs=[pl.BlockSpec((tm, k), lambda i, j: (i, 0)),
                  pl.BlockSpec((k, tn), lambda i, j: (0, j)),
                  pl.BlockSpec((tm, LANES), lambda i, j: (i % nseq, 0)),
                  pl.BlockSpec((tm, LANES), lambda i, j: (i % nseq, 0))],
        out_specs=pl.BlockSpec((tm, tn), lambda i, j: (i, j)),
        out_shape=jax.ShapeDtypeStruct((t, n), BF16),
        compiler_params=_cparams(("parallel", "parallel"), 40 << 20),
        name="q_up",
    )(cq, w_q, rope_c, rope_s)


def _kv_up_kernel(a_ref, wk_ref, wv_ref, kpe_ref, k_ref, v_ref, *, heads):
    a = a_ref[...]
    kn = _dot(a, wk_ref[...])
    vv = _dot(a, wv_ref[...])
    kpe = kpe_ref[...]
    ones = jnp.ones((a.shape[0], V_PAD - V_HEAD), v_ref.dtype)
    for h in range(heads):
        k_ref[:, h * QK_PAD:h * QK_PAD + LANES] = kn[:, h * LANES:(h + 1) * LANES].astype(k_ref.dtype)
        k_ref[:, h * QK_PAD + LANES:(h + 1) * QK_PAD] = kpe
        v_ref[:, h * V_PAD:h * V_PAD + V_HEAD] = vv[:, h * V_HEAD:(h + 1) * V_HEAD].astype(v_ref.dtype)
        v_ref[:, h * V_PAD + V_HEAD:(h + 1) * V_PAD] = ones


def kv_up(ckv, w_k, w_v, kpe, tm=1024, heads=8):
    t, k = ckv.shape
    n = w_k.shape[1]
    tn = heads * LANES
    return pl.pallas_call(
        functools.partial(_kv_up_kernel, heads=heads),
        grid=(t // tm, n // tn),
        in_specs=[pl.BlockSpec((tm, k), lambda i, j: (i, 0)),
                  pl.BlockSpec((k, tn), lambda i, j: (0, j)),
                  pl.BlockSpec((k, tn), lambda i, j: (0, j)),
                  pl.BlockSpec((tm, LANES), lambda i, j: (i, 0))],
        out_specs=[pl.BlockSpec((tm, heads * QK_PAD), lambda i, j: (i, j)),
                   pl.BlockSpec((tm, heads * V_PAD), lambda i, j: (i, j))],
        out_shape=[jax.ShapeDtypeStruct((t, MLA_HEADS * QK_PAD), BF16),
                   jax.ShapeDtypeStruct((t, MLA_HEADS * V_PAD), BF16)],
        compiler_params=_cparams(("parallel", "parallel"), 40 << 20),
        name="kv_up",
    )(ckv, w_k, w_v, kpe)


def _attn_kernel(q_ref, k_ref, v_ref, o_ref, *, tk):
    q = q_ref[...]
    nk = k_ref.shape[0] // tk
    c = QK_HEAD ** -0.5 * LOG2E

    def scores(j):
        return lax.dot_general(q, k_ref[j * tk:(j + 1) * tk, :], (((1,), (1,)), ((), ())),
                               preferred_element_type=F32)

    s_next = scores(0)
    m = acc = None
    for j in range(nk):
        s = s_next
        if j + 1 < nk:
            s_next = scores(j + 1)
        m_new = jnp.max(s, axis=-1, keepdims=True)
        if j > 0:
            m_new = jnp.maximum(m, m_new)
        p = jnp.exp2((s - m_new) * c).astype(BF16)
        pv = _dot(p, v_ref[j * tk:(j + 1) * tk, :])
        acc = pv if j == 0 else jnp.exp2((m - m_new) * c) * acc + pv
        m = m_new
    o_ref[...] = (acc[:, :V_HEAD] / acc[:, V_HEAD:]).astype(o_ref.dtype)


def attention(q, k, v, batch, seq, tq=1024, tk=1024):
    t = q.shape[0]
    nq = seq // tq
    return pl.pallas_call(
        functools.partial(_attn_kernel, tk=tk),
        grid=(batch, MLA_HEADS, nq),
        in_specs=[pl.BlockSpec((tq, QK_PAD), lambda b, h, i: (b * nq + i, h)),
                  pl.BlockSpec((seq, QK_PAD), lambda b, h, i: (b, h)),
                  pl.BlockSpec((seq, V_PAD), lambda b, h, i: (b, h))],
        out_specs=pl.BlockSpec((tq, V_HEAD), lambda b, h, i: (b * nq + i, h)),
        out_shape=jax.ShapeDtypeStruct((t, MLA_HEADS * V_HEAD), BF16),
        compiler_params=_cparams(("parallel", "parallel", "arbitrary"), 48 << 20),
        name="attention",
    )(q, k, v)


def _merge_kernel(al_ref, wl_ref, am_ref, wm_ref, ga_ref, gb_ref, o_ref):
    o_lru = _dot(al_ref[...], wl_ref[...])
    o_mla = _dot(am_ref[...], wm_ref[...])
    merged = jax.nn.sigmoid(ga_ref[...]) * o_lru + jax.nn.sigmoid(gb_ref[...]) * o_mla
    o_ref[...] = merged.astype(o_ref.dtype)


def merge_proj(a_lru, w_lru, attn, w_mla, z_gate, tm=1024, tn=256):
    m, k = attn.shape
    n = w_mla.shape[1]
    gb0 = D_MODEL // tn
    return pl.pallas_call(
        _merge_kernel,
        grid=(m // tm, n // tn),
        in_specs=[pl.BlockSpec((tm, k), lambda i, j: (i, 0)),
                  pl.BlockSpec((k, tn), lambda i, j: (0, j)),
                  pl.BlockSpec((tm, k), lambda i, j: (i, 0)),
                  pl.BlockSpec((k, tn), lambda i, j: (0, j)),
                  pl.BlockSpec((tm, tn), lambda i, j: (i, j)),
                  pl.BlockSpec((tm, tn), lambda i, j: (i, gb0 + j))],
        out_specs=pl.BlockSpec((tm, tn), lambda i, j: (i, j)),
        out_shape=jax.ShapeDtypeStruct((m, n), BF16),
        compiler_params=_cparams(("parallel", "parallel"), 56 << 20),
        name="merge_proj",
    )(a_lru, w_lru, attn, w_mla, z_gate, z_gate)


def _mm_res_kernel(a_ref, w_ref, r_ref, o_ref):
    o_ref[...] = r_ref[...] + _dot(a_ref[...], w_ref[...])


def matmul_residual(a, w, res, tm=1024, tn=1024):
    m, k = a.shape
    n = w.shape[1]
    return pl.pallas_call(
        _mm_res_kernel,
        grid=(m // tm, n // tn),
        in_specs=[pl.BlockSpec((tm, k), lambda i, j: (i, 0)),
                  pl.BlockSpec((k, tn), lambda i, j: (0, j)),
                  pl.BlockSpec((tm, tn), lambda i, j: (i, j))],
        out_specs=pl.BlockSpec((tm, tn), lambda i, j: (i, j)),
        out_shape=jax.ShapeDtypeStruct((m, n), F32),
        compiler_params=_cparams(("parallel", "parallel"), 56 << 20),
        name="matmul_residual",
    )(a, w, res)


def _up_kernel(a_ref, w_ref, o_ref):
    o_ref[...] = jnp.square(jnp.maximum(_dot(a_ref[...], w_ref[...]), 0.0)).astype(o_ref.dtype)


def mlp_up(a, w, tm=1024, tn=1024):
    m, k = a.shape
    n = w.shape[1]
    return pl.pallas_call(
        _up_kernel,
        grid=(m // tm, n // tn),
        in_specs=[pl.BlockSpec((tm, k), lambda i, j: (i, 0)),
                  pl.BlockSpec((k, tn), lambda i, j: (0, j))],
        out_specs=pl.BlockSpec((tm, tn), lambda i, j: (i, j)),
        out_shape=jax.ShapeDtypeStruct((m, n), BF16),
        compiler_params=_cparams(("parallel", "parallel"), 56 << 20),
        name="mlp_up",
    )(a, w)


def _down_kernel(a_ref, w_ref, r_ref, o_ref):
    kk = pl.program_id(2)

    @pl.when(kk == 0)
    def _():
        o_ref[...] = r_ref[...] + _dot(a_ref[...], w_ref[...])

    @pl.when(kk > 0)
    def _():
        o_ref[...] += _dot(a_ref[...], w_ref[...])


def mlp_down(a, w, res, tm=1024, tn=1024, tk=4096):
    m, k = a.shape
    n = w.shape[1]
    return pl.pallas_call(
        _down_kernel,
        grid=(m // tm, n // tn, k // tk),
        in_specs=[pl.BlockSpec((tm, tk), lambda i, j, kk: (i, kk)),
                  pl.BlockSpec((tk, tn), lambda i, j, kk: (kk, j)),
                  pl.BlockSpec((tm, tn), lambda i, j, kk: (i, j))],
        out_specs=pl.BlockSpec((tm, tn), lambda i, j, kk: (i, j)),
        out_shape=jax.ShapeDtypeStruct((m, n), F32),
        compiler_params=_cparams(("parallel", "parallel", "arbitrary"), 56 << 20),
        name="mlp_down",
    )(a, w, res)


def _rope_tables(seq):
    inv = 1.0 / (ROPE_THETA ** (jnp.arange(0, QK_ROPE, 2, dtype=F32) / QK_ROPE))
    ang = jnp.arange(seq, dtype=F32)[:, None] * inv[None, :]
    cos, sin = jnp.cos(ang), jnp.sin(ang)
    zero = jnp.zeros((seq, LANES - QK_ROPE), F32)
    return (jnp.concatenate([cos, cos, zero], axis=1),
            jnp.concatenate([-sin, sin, zero], axis=1))


def _prep_weights(w_in, lru_wa, lru_ba, lru_wx, lru_bx, lru_lam, w_q_up, w_kv_up,
                  w_lru_proj, w_mla_proj, w_out):
    d = D_MODEL
    small_lo, small_hi = 2 * d, 2 * d + Q_LORA + KV_LORA + QK_ROPE
    small = w_in[:, small_lo:small_hi]
    w_small = jnp.pad(small, ((0, 0), (0, SMALL_COLS - small.shape[1]))).astype(BF16)
    wq = w_q_up.reshape(Q_LORA, MLA_HEADS, QK_HEAD)
    wq = jnp.pad(wq, ((0, 0), (0, 0), (0, QK_PAD - QK_HEAD))).reshape(Q_LORA, MLA_HEADS * QK_PAD)
    wkv = w_kv_up.reshape(KV_LORA, MLA_HEADS, QK_NOPE + V_HEAD)
    wk = wkv[:, :, :QK_NOPE].reshape(KV_LORA, MLA_HEADS * QK_NOPE)
    wv = wkv[:, :, QK_NOPE:].reshape(KV_LORA, MLA_HEADS * V_HEAD)
    shp = (2, LRU_HEADS, 1, LRU_BLOCK)
    return dict(
        w_lru_in=w_in[:, :small_lo].astype(BF16), w_gate_in=w_in[:, small_hi:].astype(BF16),
        w_small=w_small, wq=wq.astype(BF16), wk=wk.astype(BF16), wv=wv.astype(BF16),
        wa=lru_wa.astype(BF16), wx=lru_wx.astype(BF16),
        ba=lru_ba.reshape(shp), bx=lru_bx.reshape(shp), lam=lru_lam.reshape(shp),
        w_lru_proj=w_lru_proj.astype(BF16), w_mla_proj=w_mla_proj.astype(BF16),
        w_out=w_out.astype(BF16))


def _proj_and_cast(xn, w, side, name, tile=1024):
    steps = (xn.shape[0] // tile) * (w.shape[1] // tile)
    if steps == (side.shape[0] // tile) * (side.shape[1] // tile):
        return matmul_and_cast(xn, w, side, F32, tm=tile, tn=tile, ts=tile, name=name)
    return matmul(xn, w, F32, tm=tile, tn=tile, name=name), side.astype(BF16)


def _layer(x3, p, norm1, conv_w, conv_b, q_norm, kv_norm, norm2, norm_f, w_up, w_down,
           xn=None, next_x=None, prev_h2=None):
    batch, seq, d = x3.shape
    x = x3.reshape(batch * seq, d)
    rope_c, rope_s = _rope_tables(seq)
    extras = {}

    if xn is None:
        xn = rmsnorm(x, norm1, BF16)
    if w_up.dtype != BF16:
        z_lru, w_up = _proj_and_cast(xn, p["w_lru_in"], w_up, "in_proj_lru")
        z_gate, w_down = _proj_and_cast(xn, p["w_gate_in"], w_down, "in_proj_gate")
    else:
        if prev_h2 is not None and _norm_can_ride(xn.shape[0], p["w_lru_in"].shape[1], prev_h2.shape[0], row_align=8):
            z_lru, extras["prev_y"] = matmul_and_norm(xn, p["w_lru_in"], prev_h2, norm_f, F32, F32,
                                                      name="in_proj_lru")
        else:
            z_lru = matmul(xn, p["w_lru_in"], F32, name="in_proj_lru")
        z_gate = matmul(xn, p["w_gate_in"], F32, name="in_proj_gate")
    if prev_h2 is not None and "prev_y" not in extras:
        extras["prev_y"] = rmsnorm(prev_h2, norm_f, F32)
    cq, ckv, kpe = small_proj(xn, p["w_small"], q_norm, kv_norm, rope_c, rope_s, seq)

    a_lru = lru_branch(z_lru, conv_w, conv_b, p["wa"], p["ba"], p["wx"], p["bx"], p["lam"], batch, seq)

    q = q_up(cq, p["wq"], rope_c, rope_s, seq)
    k, v = kv_up(ckv, p["wk"], p["wv"], kpe)
    attn = attention(q, k, v, batch, seq, tk=min(2048, seq // 2))

    merged = merge_proj(a_lru, p["w_lru_proj"], attn, p["w_mla_proj"], z_gate)
    h = matmul_residual(merged, p["w_out"], x)

    hn = rmsnorm(h, norm2, BF16)
    if next_x is not None and _norm_can_ride(hn.shape[0], w_up.shape[1], next_x.shape[0]):
        u, extras["next_xn"] = matmul_and_norm(hn, w_up, next_x, norm1, BF16, BF16, relu2=True, name="mlp_up")
    else:
        u = mlp_up(hn, w_up)
        if next_x is not None:
            extras["next_xn"] = rmsnorm(next_x, norm1, BF16)
    h2 = mlp_down(u, w_down, h)
    return h2, w_up, w_down, extras


def kernel(x_prompt, x_sample, norm1, w_in, conv_w, conv_b, lru_wa, lru_ba, lru_wx, lru_bx, lru_lam,
           q_norm, w_q_up, kv_norm, w_kv_up, w_lru_proj, w_mla_proj, w_out, norm2, w_up, w_down, norm_f):
    assert norm1.shape[0] == 1, "single-layer trunk"
    p = _prep_weights(w_in[0], lru_wa[0], lru_ba[0], lru_wx[0], lru_bx[0], lru_lam[0], w_q_up[0],
                      w_kv_up[0], w_lru_proj[0], w_mla_proj[0], w_out[0])
    args = (p, norm1[0], conv_w[0], conv_b[0], q_norm[0], kv_norm[0], norm2[0], norm_f)
    d = x_prompt.shape[-1]
    h2_prompt, w_up_bf16, w_down_bf16, ex1 = _layer(x_prompt, *args, w_up[0], w_down[0],
                                                    next_x=x_sample.reshape(-1, d))
    h2_sample, _, _, ex2 = _layer(x_sample, *args, w_up_bf16, w_down_bf16,
                                  xn=ex1["next_xn"], prev_h2=h2_prompt)
    y_prompt = ex2["prev_y"].reshape(x_prompt.shape)
    y_sample = rmsnorm(h2_sample, norm_f, F32).reshape(x_sample.shape)
    return (y_prompt, y_sample)
```
